```python
import math
import jax
import jax.numpy as jnp
from jax import lax
import numpy as np

D_MODEL = 2048
BATCH = 4
SEQ = 2048
DEPTH = 2

CTX_LEN = 256
GRID_W = 64
N_SUB = 3
D_FF = 5632
FFN_RES = 0.5
HGRN_HEADS = 8
HGRN_DK = 128
HGRN_DV = 128
DN_HEADS = 8
DN_DK = 128
DN_DV = 128
DN_CONV = 5
CHUNK = 64
EPS = 1e-6
KA = HGRN_HEADS * HGRN_DK
WA = HGRN_HEADS * HGRN_DV
KB = DN_HEADS * DN_DK
WB = DN_HEADS * DN_DV
QKV_B = 2 * KB + WB
IN_SPLITS = (KA, KA, KA, WA, WA, QKV_B, WB, 2 * DN_HEADS, 2 * DN_HEADS, D_MODEL, D_MODEL)
P_IN = 3 * KA + 2 * WA + QKV_B + WB + 4 * DN_HEADS + 2 * D_MODEL

kernel_name = 'hybrid_hgrn2_gdn_macaron_dit'


def _rms_norm(u, g):
    uf = u.astype(jnp.float32)
    y = uf * lax.rsqrt(jnp.mean(uf * uf, axis=-1, keepdims=True) + EPS)
    return (y * g.astype(jnp.float32)).astype(u.dtype)


def _l2norm(u):
    uf = u.astype(jnp.float32)
    return uf * lax.rsqrt(jnp.sum(uf * uf, axis=-1, keepdims=True) + EPS)


def _heads(u, n_heads):
    bsz, length, _ = u.shape
    return u.reshape(bsz, length, n_heads, -1).transpose(0, 2, 1, 3)


def _unheads(o):
    bsz, nh, length, d = o.shape
    return o.transpose(0, 2, 1, 3).reshape(bsz, length, nh * d)


def _adaln(cvec, w, b):
    m = jax.nn.silu(cvec) @ w + b
    m = m.reshape(cvec.shape[:-1] + (N_SUB, 3, 1, D_MODEL))
    return jnp.moveaxis(jnp.moveaxis(m, -4, 0), -3, 1)


def _ffn_sublayer(s, g, shift, scale, gate, wg, wu, wd):
    h = _rms_norm(s, g) * (1 + scale) + shift
    return s + FFN_RES * gate * ((jax.nn.silu(h @ wg) * (h @ wu)) @ wd)


def _split_cols(p):
    out = []
    start = 0
    for size in IN_SPLITS:
        out.append(p[..., start:start + size])
        start += size
    return out


def _short_conv(u, w, grid):
    bsz, length, ch = u.shape
    if grid:
        rows = length // GRID_W
        u = u.reshape(bsz * rows, GRID_W, ch)
    y = lax.conv_general_dilated(u, w.astype(u.dtype)[:, None, :], window_strides=(1,),
                                 padding=[(DN_CONV // 2, DN_CONV // 2)],
                                 dimension_numbers=('NWC', 'WIO', 'NWC'), feature_group_count=ch)
    return y.reshape(bsz, length, ch)


def _hgrn_gate(f_logit, lb):
    ff = f_logit.astype(jnp.float32)
    logf = jnp.log(lb + (1.0 - lb) * jax.nn.sigmoid(ff))
    k = (1.0 - lb) * jax.nn.sigmoid(-ff)
    return logf, k


def _masked_exp(mask, z):
    return jnp.where(mask, jnp.exp(jnp.where(mask, z, 0.0)), 0.0)


def _hgrn2_scan(q, k, v, logf, s0):
    bsz, nh, length, _ = q.shape
    n = length // CHUNK
    tril = jnp.tril(jnp.ones((CHUNK, CHUNK), dtype=bool))

    def to_chunks(u):
        return jnp.moveaxis(u.reshape(bsz, nh, n, CHUNK, u.shape[-1]), 2, 0)

    def step(s, inp):
        qc, kc, vc, lc = inp
        b = jnp.cumsum(lc, axis=2)
        decay = _masked_exp(tril[:, :, None], b[:, :, :, None, :] - b[:, :, None, :, :])
        attn = jnp.einsum('bhtk,bhsk,bhtsk->bhts', qc, kc, decay)
        o = (jnp.einsum('bhtk,bhkv->bhtv', qc * jnp.exp(b), s)
             + jnp.einsum('bhts,bhsv->bhtv', attn, vc))
        b_last = b[:, :, -1:, :]
        s = (jnp.exp(b_last[:, :, 0, :])[..., None] * s
             + jnp.einsum('bhsk,bhsv->bhkv', kc * jnp.exp(b_last - b), vc))
        return s, o

    s_fin, o = lax.scan(step, s0, (to_chunks(q), to_chunks(k), to_chunks(v), to_chunks(logf)))
    return jnp.moveaxis(o, 0, 2).reshape(bsz, nh, length, -1), s_fin


def _gated_delta_scan(q, k, v, g, beta, s0):
    bsz, nh, length, _ = q.shape
    dv = v.shape[-1]
    n = length // CHUNK
    q, k, v = (u.reshape(bsz, nh, n, CHUNK, u.shape[-1]) for u in (q, k, v))
    g, beta = (u.reshape(bsz, nh, n, CHUNK) for u in (g, beta))
    tril = jnp.tril(jnp.ones((CHUNK, CHUNK), dtype=bool))
    strict = jnp.tril(jnp.ones((CHUNK, CHUNK), dtype=bool), -1)
    gc = jnp.cumsum(g, axis=-1)
    gamma = _masked_exp(tril, gc[..., :, None] - gc[..., None, :])
    kb = k * beta[..., None]
    a = jnp.where(strict, jnp.einsum('bhntk,bhnsk->bhnts', kb, k) * gamma, 0.0)
    lhs = a + jnp.eye(CHUNK, dtype=a.dtype)
    rhs = jnp.concatenate([v * beta[..., None], kb * jnp.exp(gc)[..., None]], axis=-1)
    sol = lax.linalg.triangular_solve(lhs, rhs, left_side=True, lower=True, unit_diagonal=True)
    u_c, w_c = sol[..., :dv], sol[..., dv:]
    qk = jnp.einsum('bhntk,bhnsk->bhnts', q, k) * gamma
    qg = q * jnp.exp(gc)[..., None]
    kd = k * jnp.exp(gc[..., -1:] - gc)[..., None]
    dl = jnp.exp(gc[..., -1])

    def step(s, inp):
        u_n, w_n, qk_n, qg_n, kd_n, dl_n = inp
        v_new = u_n - jnp.einsum('bhtk,bhkv->bhtv', w_n, s)
        o = jnp.einsum('bhtk,bhkv->bhtv', qg_n, s) + jnp.einsum('bhts,bhsv->bhtv', qk_n, v_new)
        s = dl_n[..., None, None] * s + jnp.einsum('bhsk,bhsv->bhkv', kd_n, v_new)
        return s, o

    xs = tuple(jnp.moveaxis(t, 2, 0) for t in (u_c, w_c, qk, qg, kd, dl))
    s_fin, o = lax.scan(step, s0, xs)
    return jnp.moveaxis(o, 0, 2).reshape(bsz, nh, length, dv), s_fin


def _flip(arrs):
    return tuple(jnp.flip(t, axis=2) for t in arrs)


def _bidir(core, ctx_dirs, lat_dirs, s0):
    (cf, cb), (xf, xb) = ctx_dirs, lat_dirs
    oc_f, sc_f = core(*cf, s0)
    oc_b, sc_b = core(*_flip(cb), s0)
    ox_f, _ = core(*xf, sc_f)
    ox_b, _ = core(*_flip(xb), sc_b)
    return oc_f + jnp.flip(oc_b, axis=2), ox_f + jnp.flip(ox_b, axis=2)


def _mixer_project(h, w_in, conv_w, lb, a_log, dt_bias, grid):
    f32 = jnp.float32
    aq, af_f, af_b, av, ag, bqkv, bz, ba, bb, ga, gb = _split_cols(h @ w_in)
    qa = _heads(jax.nn.silu(aq), HGRN_HEADS).astype(f32)
    va = _heads(av, HGRN_HEADS).astype(f32)
    lf_f, ka_f = _hgrn_gate(af_f, lb[0])
    lf_b, ka_b = _hgrn_gate(af_b, lb[1])
    hg = ((qa, _heads(ka_f, HGRN_HEADS), va, _heads(lf_f, HGRN_HEADS)),
          (qa, _heads(ka_b, HGRN_HEADS), va, _heads(lf_b, HGRN_HEADS)))
    qkv = jax.nn.silu(_short_conv(bqkv, conv_w, grid))
    qb = _l2norm(_heads(qkv[..., :KB], DN_HEADS)) * (DN_DK ** -0.5)
    kb = _l2norm(_heads(qkv[..., KB:2 * KB], DN_HEADS))
    vb = _heads(qkv[..., 2 * KB:], DN_HEADS).astype(f32)
    bsz, length, _ = h.shape
    a_in = ba.astype(f32).reshape(bsz, length, 2, DN_HEADS)
    g = -jnp.exp(a_log.astype(f32)) * jax.nn.softplus(a_in + dt_bias.astype(f32))
    g = g.transpose(2, 0, 3, 1)
    beta = jax.nn.sigmoid(bb.astype(f32).reshape(bsz, length, 2, DN_HEADS)).transpose(2, 0, 3, 1)
    dn = ((qb, kb, vb, g[0], beta[0]), (qb, kb, vb, g[1], beta[1]))
    return hg, dn, (ag, bz, ga, gb)


def _merge(oa, ob, gates, hgrn_g, dn_g, w_br_a, w_br_b, w_out, dtype):
    ag, bz, ga, gb = gates
    ya = _unheads(_rms_norm(oa, hgrn_g)).astype(dtype) * jax.nn.silu(ag)
    yb = _unheads(_rms_norm(ob, dn_g)).astype(dtype) * jax.nn.silu(bz)
    y = jax.nn.sigmoid(ga) * (ya @ w_br_a) + jax.nn.sigmoid(gb) * (yb @ w_br_b)
    return y @ w_out


def _mixer(hc, hx, w_in, conv_w, lb, a_log, dt_bias, hgrn_g, dn_g, w_br_a, w_br_b, w_out, need_ctx):
    hg_c, dn_c, gt_c = _mixer_project(hc, w_in, conv_w, lb, a_log, dt_bias, False)
    hg_x, dn_x, gt_x = _mixer_project(hx, w_in, conv_w, lb, a_log, dt_bias, True)
    bsz = hx.shape[0]
    zero_a = jnp.zeros((bsz, HGRN_HEADS, HGRN_DK, HGRN_DV), jnp.float32)
    zero_b = jnp.zeros((bsz, DN_HEADS, DN_DK, DN_DV), jnp.float32)
    oa_c, oa_x = _bidir(_hgrn2_scan, hg_c, hg_x, zero_a)
    ob_c, ob_x = _bidir(_gated_delta_scan, dn_c, dn_x, zero_b)
    y_x = _merge(oa_x, ob_x, gt_x, hgrn_g, dn_g, w_br_a, w_br_b, w_out, hx.dtype)
    y_c = _merge(oa_c, ob_c, gt_c, hgrn_g, dn_g, w_br_a, w_br_b, w_out, hc.dtype) if need_ctx else None
    return y_c, y_x


def setup_inputs(seed: int = 0) -> dict:
    key = jax.random.key(seed)
    ks = jax.random.split(key, 22)
    f32 = jnp.float32
    D = D_MODEL

    def nrm(k, shape, fan_in):
        return jax.random.normal(k, shape, f32) * (fan_in ** -0.5)

    dt = jnp.exp(jax.random.uniform(ks[15], (DEPTH, 2, DN_HEADS), f32)
                 * (math.log(0.1) - math.log(0.001)) + math.log(0.001))
    return {
        'x': jax.random.normal(ks[0], (BATCH, SEQ, D), f32),
        'c': jax.random.normal(ks[1], (BATCH, D), f32),
        'ctx': jax.random.normal(ks[2], (BATCH, CTX_LEN, D), f32),
        'c_ctx': jax.random.normal(ks[3], (D,), f32),
        'w_ada': nrm(ks[4], (DEPTH, D, N_SUB * 3 * D), D) * 0.5,
        'b_ada': 0.02 * jax.random.normal(ks[5], (DEPTH, N_SUB * 3 * D), f32),
        'norm_g': 1.0 + 0.05 * jax.random.normal(ks[6], (DEPTH, N_SUB, D), f32),
        'final_norm_g': 1.0 + 0.05 * jax.random.normal(ks[7], (D,), f32),
        'ffn_w_gate': nrm(ks[8], (DEPTH, 2, D, D_FF), D),
        'ffn_w_up': nrm(ks[9], (DEPTH, 2, D, D_FF), D),
        'ffn_w_down': nrm(ks[10], (DEPTH, 2, D_FF, D), D_FF),
        'w_in': nrm(ks[11], (DEPTH, D, P_IN), D),
        'hgrn_lower_bounds': 0.5 * jax.random.normal(ks[12], (DEPTH, 2, KA), f32),
        'hgrn_norm_g': 1.0 + 0.05 * jax.random.normal(ks[13], (DEPTH, HGRN_DV), f32),
        'dn_conv_w': jax.random.normal(ks[14], (DEPTH, DN_CONV, QKV_B), f32) * (DN_CONV ** -0.5),
        'dn_a_log': jnp.log(jax.random.uniform(ks[16], (DEPTH, 2, DN_HEADS), f32, minval=1.0, maxval=16.0)),
        'dn_dt_bias': dt + jnp.log(-jnp.expm1(-dt)),
        'dn_norm_g': 1.0 + 0.05 * jax.random.normal(ks[17], (DEPTH, DN_DV), f32),
        'w_branch_a': nrm(ks[18], (DEPTH, WA, D), WA),
        'w_branch_b': nrm(ks[19], (DEPTH, WB, D), WB),
        'w_out': nrm(ks[20], (DEPTH, D, D), D),
    }


def reference(x, c, ctx, c_ctx, w_ada, b_ada, norm_g, final_norm_g, ffn_w_gate, ffn_w_up, ffn_w_down,
              w_in, hgrn_lower_bounds, hgrn_norm_g, dn_conv_w, dn_a_log, dn_dt_bias, dn_norm_g,
              w_branch_a, w_branch_b, w_out):
    lbs = jax.nn.softmax(hgrn_lower_bounds.astype(jnp.float32), axis=0)
    lb_all = jnp.cumsum(lbs, axis=0) - lbs[0]
    cx = ctx
    for l in range(DEPTH):
        last = l == DEPTH - 1
        mx = _adaln(c, w_ada[l], b_ada[l])
        mc = _adaln(c_ctx, w_ada[l], b_ada[l])
        x = _ffn_sublayer(x, norm_g[l, 0], mx[0, 0], mx[0, 1], mx[0, 2],
                          ffn_w_gate[l, 0], ffn_w_up[l, 0], ffn_w_down[l, 0])
        cx = _ffn_sublayer(cx, norm_g[l, 0], mc[0, 0], mc[0, 1], mc[0, 2],
                           ffn_w_gate[l, 0], ffn_w_up[l, 0], ffn_w_down[l, 0])
        hx = _rms_norm(x, norm_g[l, 1]) * (1 + mx[1, 1]) + mx[1, 0]
        hc = _rms_norm(cx, norm_g[l, 1]) * (1 + mc[1, 1]) + mc[1, 0]
        y_c, y_x = _mixer(hc, hx, w_in[l], dn_conv_w[l], lb_all[l], dn_a_log[l], dn_dt_bias[l],
                          hgrn_norm_g[l], dn_norm_g[l], w_branch_a[l], w_branch_b[l], w_out[l],
                          not last)
        x = x + mx[1, 2] * y_x
        x = _ffn_sublayer(x, norm_g[l, 2], mx[2, 0], mx[2, 1], mx[2, 2],
                          ffn_w_gate[l, 1], ffn_w_up[l, 1], ffn_w_down[l, 1])
        if not last:
            cx = cx + mc[1, 2] * y_c
            cx = _ffn_sublayer(cx, norm_g[l, 2], mc[2, 0], mc[2, 1], mc[2, 2],
                               ffn_w_gate[l, 1], ffn_w_up[l, 1], ffn_w_down[l, 1])
    return _rms_norm(x, final_norm_g)
```

```python
import functools

import jax
import jax.numpy as jnp
from jax import lax
from jax.experimental import pallas as pl
from jax.experimental.pallas import tpu as pltpu

EPS = 1e-6
FFN_RES = 0.5
HEAD_DIM = 128
CHUNK = 64
SUB = 8
LANES = 128
VMEM_LIMIT_BYTES = 56 * 1024 * 1024

f32 = jnp.float32
bf16 = jnp.bfloat16


def _cparams(*sem):
    return pltpu.CompilerParams(dimension_semantics=sem, vmem_limit_bytes=VMEM_LIMIT_BYTES)


def _bf(a):
    return a.astype(bf16)


def _dot(a, b):
    return jnp.dot(_bf(a), _bf(b), preferred_element_type=f32)


def _dot_nt(a, b):
    return lax.dot_general(_bf(a), _bf(b), (((1,), (1,)), ((), ())), preferred_element_type=f32)


def _dot_tn(a, b):
    return lax.dot_general(_bf(a), _bf(b), (((0,), (0,)), ((), ())), preferred_element_type=f32)


def _split2(a):
    hi = _bf(a)
    lo = _bf(a - hi.astype(f32))
    return hi, lo


def _split3(a):
    hi = _bf(a)
    r = a - hi.astype(f32)
    mid = _bf(r)
    lo = _bf(r - mid.astype(f32))
    return hi, mid, lo


def _dot_sel(m01, a):
    hi, mid, lo = _split3(a)
    d = functools.partial(jnp.dot, preferred_element_type=f32)
    return d(m01, hi) + d(m01, mid) + d(m01, lo)


def _dot_hi(a, b):
    ah, al = _split2(a)
    bh, bl = _split2(b)
    d = functools.partial(jnp.dot, preferred_element_type=f32)
    return d(ah, bh) + d(ah, bl) + d(al, bh)


def _sigmoid(a):
    return 1.0 / (1.0 + jnp.exp(-a))


def _silu(a):
    return a * _sigmoid(a)


def _largest_tile(total, unit, cap):
    n = total // unit
    best = 1
    for k in range(1, n + 1):
        if n % k == 0 and k * unit <= cap:
            best = k
    return best * unit


def _adaln_kernel(c_ref, w_ref, b_ref, o_ref):
    a = _silu(c_ref[...])
    o_ref[...] = _dot(a, w_ref[...]) + b_ref[...]


def _adaln(cc, w_ada, b_ada):
    depth, d, n = w_ada.shape
    rows = cc.shape[0]
    tn = _largest_tile(n, LANES, 1024)
    return pl.pallas_call(
        _adaln_kernel,
        grid=(depth, n // tn),
        in_specs=[pl.BlockSpec((rows, d), lambda l, j: (0, 0)),
                  pl.BlockSpec((None, d, tn), lambda l, j: (l, 0, j)),
                  pl.BlockSpec((None, 1, tn), lambda l, j: (l, 0, j))],
        out_specs=pl.BlockSpec((None, rows, tn), lambda l, j: (l, 0, j)),
        out_shape=jax.ShapeDtypeStruct((depth, rows, n), f32),
        compiler_params=_cparams("parallel", "parallel"),
    )(cc, w_ada, b_ada.reshape(depth, 1, n))


def _is_ctx_rows(tile_idx, tiles_per_batch, tm, ctx_len):
    row0 = (tile_idx % tiles_per_batch) * tm
    rows = row0 + lax.broadcasted_iota(jnp.int32, (tm, 1), 0)
    return rows < ctx_len


def _norm_mod(s, g, mod, is_ctx):
    y = s * lax.rsqrt(jnp.mean(s * s, axis=-1, keepdims=True) + EPS) * g
    shift = jnp.where(is_ctx, mod[3:4], mod[0:1])
    scale = jnp.where(is_ctx, mod[4:5], mod[1:2])
    return y * (1.0 + scale) + shift


def _ffn_kernel(s_ref, g_ref, mod_ref, wg_ref, wu_ref, wd_ref, o_ref, h_ref, *, tiles_per_batch, ctx_len):
    i = pl.program_id(0)
    f = pl.program_id(1)
    tm = s_ref.shape[0]
    is_ctx = _is_ctx_rows(i, tiles_per_batch, tm, ctx_len)

    @pl.when(f == 0)
    def _():
        h_ref[...] = _bf(_norm_mod(s_ref[...], g_ref[...], mod_ref[...], is_ctx))
        o_ref[...] = jnp.zeros_like(o_ref)

    h = h_ref[...]
    a = jnp.dot(h, wg_ref[...], preferred_element_type=f32)
    u = jnp.dot(h, wu_ref[...], preferred_element_type=f32)
    o_ref[...] += _dot(_silu(a) * u, wd_ref[...])

    @pl.when(f == pl.num_programs(1) - 1)
    def _():
        mod = mod_ref[...]
        gate = jnp.where(is_ctx, mod[5:6], mod[2:3])
        o_ref[...] = s_ref[...] + FFN_RES * gate * o_ref[...]


def _ffn(s, g, mods, wg, wu, wd, layer, sub, which, lb, ctx_len):
    t, d = s.shape
    dff = wg.shape[-1]
    tm = _largest_tile(lb, SUB, 576)
    tf = _largest_tile(dff, LANES, 512)
    tpb = lb // tm
    kern = functools.partial(_ffn_kernel, tiles_per_batch=tpb, ctx_len=ctx_len)
    return pl.pallas_call(
        kern,
        grid=(t // tm, dff // tf),
        in_specs=[pl.BlockSpec((tm, d), lambda i, f: (i, 0)),
                  pl.BlockSpec((None, None, 1, d), lambda i, f: (layer, sub, 0, 0)),
                  pl.BlockSpec((None, None, None, 8, d), lambda i, f: (layer, sub, i // tpb, 0, 0)),
                  pl.BlockSpec((None, None, d, tf), lambda i, f: (layer, which, 0, f)),
                  pl.BlockSpec((None, None, d, tf), lambda i, f: (layer, which, 0, f)),
                  pl.BlockSpec((None, None, tf, d), lambda i, f: (layer, which, f, 0))],
        out_specs=pl.BlockSpec((tm, d), lambda i, f: (i, 0)),
        out_shape=jax.ShapeDtypeStruct((t, d), f32),
        scratch_shapes=[pltpu.VMEM((tm, d), bf16)],
        compiler_params=_cparams("parallel", "arbitrary"),
    )(s, g, mods, wg, wu, wd)


def _proj_kernel(s_ref, g_ref, mod_ref, w_ref, o_ref, h_ref, *, tiles_per_batch, ctx_len):
    i = pl.program_id(0)

    @pl.when(pl.program_id(1) == 0)
    def _():
        is_ctx = _is_ctx_rows(i, tiles_per_batch, s_ref.shape[0], ctx_len)
        h_ref[...] = _bf(_norm_mod(s_ref[...], g_ref[...], mod_ref[...], is_ctx))

    o_ref[...] = jnp.dot(h_ref[...], w_ref[...], preferred_element_type=f32)


def _proj(s, g, mods, w, layer, lb, ctx_len):
    t, d = s.shape
    n = w.shape[-1]
    tm = _largest_tile(lb, SUB, 768)
    tn = _largest_tile(n, LANES, 1024)
    tpb = lb // tm
    kern = functools.partial(_proj_kernel, tiles_per_batch=tpb, ctx_len=ctx_len)
    return pl.pallas_call(
        kern,
        grid=(t // tm, n // tn),
        in_specs=[pl.BlockSpec((tm, d), lambda i, j: (i, 0)),
                  pl.BlockSpec((None, None, 1, d), lambda i, j: (layer, 1, 0, 0)),
                  pl.BlockSpec((None, None, None, 8, d), lambda i, j: (layer, 1, i // tpb, 0, 0)),
                  pl.BlockSpec((None, d, tn), lambda i, j: (layer, 0, j))],
        out_specs=pl.BlockSpec((tm, tn), lambda i, j: (i, j)),
        out_shape=jax.ShapeDtypeStruct((t, n), f32),
        scratch_shapes=[pltpu.VMEM((tm, d), bf16)],
        compiler_params=_cparams("parallel", "arbitrary"),
    )(s, g, mods, w)


def _chunk_index(i, n_ctx, n_chunks, reverse):
    if not reverse:
        return i
    return jnp.where(i < n_ctx, n_ctx - 1 - i, n_chunks + n_ctx - 1 - i)


def _causal_masks(c, reverse):
    t = lax.broadcasted_iota(jnp.int32, (c, c), 0)
    s = lax.broadcasted_iota(jnp.int32, (c, c), 1)
    if reverse:
        return s >= t, s > t, t, s
    return s <= t, s < t, t, s


def _head_rms(o, g):
    return o * lax.rsqrt(jnp.mean(o * o, axis=-1, keepdims=True) + EPS) * g


def _hgrn_chunk(q, k, v, lc, st, consts, reverse):
    c = q.shape[0]
    m_incl, level_masks, sub_mask, ones_red = consts
    b = _dot_sel(m_incl, lc)
    b_end = b[0:1] if reverse else b[c - 1:c]
    o = _dot_nt(q * jnp.exp(b), st)
    st_new = st * jnp.exp(b_end) + _dot_tn(v, k * jnp.exp(b_end - b))

    attn = jnp.zeros((c, c), f32)
    m = c // 2
    for lm in level_masks:
        n = c // m
        b3 = b.reshape(n, m, LANES)
        if reverse:
            own = b3[:, m - 1:m]
            nbr = jnp.concatenate([own[:1], own[:-1]], axis=0)
        else:
            own = b3[:, 0:1]
            nbr = jnp.concatenate([own[1:], own[-1:]], axis=0)
        qs = (q.reshape(n, m, LANES) * jnp.exp(jnp.minimum(b3 - own, 0.0))).reshape(c, LANES)
        ks = (k.reshape(n, m, LANES) * jnp.exp(jnp.minimum(nbr - b3, 0.0))).reshape(c, LANES)
        attn = attn + jnp.where(lm, _dot_nt(qs, ks), 0.0)
        m //= 2
    o = o + _dot(attn, v)

    n = c // SUB
    b3 = b.reshape(n, SUB, LANES)
    q3 = q.reshape(n, SUB, LANES)
    k3 = k.reshape(n, SUB, LANES)
    v3 = v.reshape(n, SUB, LANES)
    prods = []
    for j in range(SUB):
        e = jnp.exp(jnp.minimum(b3 - b3[:, j:j + 1], 0.0))
        p = jnp.where(sub_mask[j], q3 * e * k3[:, j:j + 1], 0.0)
        prods.append(p.reshape(c, LANES))
    sums = jnp.dot(_bf(jnp.concatenate(prods, axis=0)), ones_red, preferred_element_type=f32)
    od = jnp.zeros((n, SUB, LANES), f32)
    for j in range(SUB):
        od = od + sums[j * c:(j + 1) * c].reshape(n, SUB, LANES) * v3[:, j:j + 1]
    return o + od.reshape(c, LANES), st_new


def _hgrn_consts(c, reverse):
    m_incl, _, t, s = _causal_masks(c, reverse)
    level_masks = []
    m = c // 2
    while m >= SUB:
        tb, sb = t // m, s // m
        if reverse:
            level_masks.append((tb % 2 == 0) & (sb == tb + 1))
        else:
            level_masks.append((tb % 2 == 1) & (sb == tb - 1))
        m //= 2
    r = lax.broadcasted_iota(jnp.int32, (1, SUB, LANES), 1)
    sub_mask = [(r <= j) if reverse else (r >= j) for j in range(SUB)]
    return (jnp.where(m_incl, 1.0, 0.0).astype(bf16), level_masks, sub_mask, jnp.ones((LANES, LANES), bf16))


def _hgrn_kernel(q_ref, ff_ref, fb_ref, v_ref, lb_ref, g_ref, o_ref, *, layer, ctx_len):
    c = CHUNK
    n_chunks = q_ref.shape[0] // c
    n_ctx = ctx_len // c

    raw = lb_ref[...]
    e = jnp.exp(raw - jnp.max(raw, axis=0, keepdims=True))
    sm = e / jnp.sum(e, axis=0, keepdims=True)
    lbs = jnp.sum(sm[:layer + 1], axis=0) - sm[0]

    for reverse in (False, True):
        consts = _hgrn_consts(c, reverse)
        f_ref = fb_ref if reverse else ff_ref
        lb = lbs[1:2] if reverse else lbs[0:1]

        def body(i, st, consts=consts, f_ref=f_ref, lb=lb, reverse=reverse):
            r0 = pl.multiple_of(_chunk_index(i, n_ctx, n_chunks, reverse) * c, c)
            xq = q_ref[pl.ds(r0, c), :]
            xf = f_ref[pl.ds(r0, c), :]
            v = v_ref[pl.ds(r0, c), :]
            q = _silu(xq)
            lc = jnp.log(lb + (1.0 - lb) * _sigmoid(xf))
            k = (1.0 - lb) * _sigmoid(-xf)
            o, st = _hgrn_chunk(q, k, v, lc, st, consts, reverse)
            if reverse:
                o_ref[pl.ds(r0, c), :] += o
            else:
                o_ref[pl.ds(r0, c), :] = o
            return st

        lax.fori_loop(0, n_chunks, body, jnp.zeros((HEAD_DIM, HEAD_DIM), f32))

    o_ref[...] = _head_rms(o_ref[...], g_ref[...])


def _hgrn(p, lower_bounds, norm_g, layer, batch, lb_rows, ctx_len, heads, off_q, off_ff, off_fb, off_v):
    t = p.shape[0]
    depth = lower_bounds.shape[0]

    def col(off):
        return pl.BlockSpec((lb_rows, HEAD_DIM), lambda b, h: (b, off // HEAD_DIM + h))

    kern = functools.partial(_hgrn_kernel, layer=layer, ctx_len=ctx_len)
    return pl.pallas_call(
        kern,
        grid=(batch, heads),
        in_specs=[col(off_q), col(off_ff), col(off_fb), col(off_v),
                  pl.BlockSpec((depth, 2, HEAD_DIM), lambda b, h: (0, 0, h)),
                  pl.BlockSpec((None, 1, HEAD_DIM), lambda b, h: (layer, 0, 0))],
        out_specs=pl.BlockSpec((lb_rows, HEAD_DIM), lambda b, h: (b, h)),
        out_shape=jax.ShapeDtypeStruct((t, heads * HEAD_DIM), f32),
        compiler_params=_cparams("parallel", "parallel"),
    )(p, p, p, p, lower_bounds, norm_g.reshape(depth, 1, HEAD_DIM))


def _short_conv_tile(u, cw, width):
    r = u.shape[0]
    rows = lax.broadcasted_iota(jnp.int32, (r, 1), 0)
    half = width // 2
    acc = u * cw[half:half + 1]
    for j in range(width):
        if j == half:
            continue
        uj = pltpu.roll(u, (half - j) % r, 0)
        valid = rows >= (half - j) if j < half else rows < r - (j - half)
        acc = acc + jnp.where(valid, uj, 0.0) * cw[j:j + 1]
    return acc


def _dn_kernel(alog_ref, dt_ref, q_ref, k_ref, v_ref, ab_ref, cwq_ref, cwk_ref, cwv_ref, g_ref, o_ref,
               qn, kn, vn, u_s, w_s, qg_s, kd_s, qk_s, gl_s, *, layer, ctx_len):
    c = CHUNK
    h = pl.program_id(1)
    n_chunks = q_ref.shape[0] // c
    n_ctx = ctx_len // c
    width = cwq_ref.shape[0]

    def prep(r0, rows):
        for src, cw, dst, mode in ((q_ref, cwq_ref, qn, "q"), (k_ref, cwk_ref, kn, "k"), (v_ref, cwv_ref, vn, "v")):
            y = _silu(_short_conv_tile(src[pl.ds(r0, rows), :], cw[...], width))
            if mode != "v":
                y = y * lax.rsqrt(jnp.sum(y * y, axis=-1, keepdims=True) + EPS)
            if mode == "q":
                y = y * (HEAD_DIM ** -0.5)
            dst[pl.ds(r0, rows), :] = y

    prep(0, ctx_len)

    def prep_body(i, carry):
        prep(pl.multiple_of(ctx_len + i * c, c), c)
        return carry

    lax.fori_loop(0, n_chunks - n_ctx, prep_body, 0)

    eye = jnp.where(lax.broadcasted_iota(jnp.int32, (c, c), 0) == lax.broadcasted_iota(jnp.int32, (c, c), 1), 1.0, 0.0)
    ones_cc = jnp.ones((c, c), bf16)

    for reverse in (False, True):
        d = 1 if reverse else 0
        m_incl, m_strict, _, _ = _causal_masks(c, reverse)
        m01 = jnp.where(m_incl, 1.0, 0.0).astype(bf16)
        m_incl_t, _, _, _ = _causal_masks(c, not reverse)
        neg_a = -jnp.exp(jnp.full((1, 1), alog_ref[layer, d, h], f32))
        dt = dt_ref[layer, d, h]

        def pre_body(ci, carry, d=d, reverse=reverse, m_incl=m_incl, m_strict=m_strict, m01=m01,
                     m_incl_t=m_incl_t, neg_a=neg_a, dt=dt):
            r0 = pl.multiple_of(ci * c, c)
            q = qn[pl.ds(r0, c), :]
            k = kn[pl.ds(r0, c), :]
            v = vn[pl.ds(r0, c), :]
            ab = ab_ref[pl.ds(r0, c), :]
            za = ab[:, d:d + 1] + dt
            g = neg_a * (jnp.maximum(za, 0.0) + jnp.log(1.0 + jnp.exp(-jnp.abs(za))))
            beta = _sigmoid(ab[:, 2 + d:3 + d])
            gfull = jnp.broadcast_to(g, (c, LANES))
            gc = _dot_sel(m01, gfull)
            gc_end = gc[0:1] if reverse else gc[c - 1:c]
            egc = jnp.exp(gc)
            gc_row = _dot_sel(ones_cc, jnp.where(m_incl_t, gfull[:, :c], 0.0))
            gamma = jnp.where(m_incl, jnp.exp(jnp.minimum(gc[:, :c] - gc_row, 0.0)), 0.0)
            kb = k * beta
            kk = _dot_nt(jnp.concatenate([kb, q], axis=0), k)
            a = jnp.where(m_strict, kk[:c] * gamma, 0.0)
            pw = -a
            inv = eye + pw
            span = 2
            while span < c:
                pw = _dot_hi(pw, pw)
                inv = inv + _dot_hi(inv, pw)
                span *= 2
            sol = _dot_hi(inv, jnp.concatenate([v * beta, kb * egc], axis=1))
            u_s[pl.ds(r0, c), :] = sol[:, :HEAD_DIM]
            w_s[pl.ds(r0, c), :] = sol[:, HEAD_DIM:]
            qg_s[pl.ds(r0, c), :] = q * egc
            kd_s[pl.ds(r0, c), :] = k * jnp.exp(gc_end - gc)
            qk_s[pl.ds(r0, c), :] = kk[c:] * gamma
            gl_s[ci] = jnp.broadcast_to(gc_end, (SUB, LANES))
            return carry

        lax.fori_loop(0, n_chunks, pre_body, 0)

        def scan_body(i, st, reverse=reverse):
            ci = _chunk_index(i, n_ctx, n_chunks, reverse)
            r0 = pl.multiple_of(ci * c, c)
            ws = _dot(jnp.concatenate([w_s[pl.ds(r0, c), :], qg_s[pl.ds(r0, c), :]], axis=0), st)
            v_new = u_s[pl.ds(r0, c), :] - ws[:c]
            o = ws[c:] + _dot(qk_s[pl.ds(r0, c), :], v_new)
            st = jnp.exp(gl_s[ci][0:1]) * st + _dot_tn(kd_s[pl.ds(r0, c), :], v_new)
            if reverse:
                o_ref[pl.ds(r0, c), :] += o
            else:
                o_ref[pl.ds(r0, c), :] = o
            return st

        lax.fori_loop(0, n_chunks, scan_body, jnp.zeros((HEAD_DIM, HEAD_DIM), f32))

    o_ref[...] = _head_rms(o_ref[...], g_ref[...])


def _dn(p, ab, conv_w, a_log, dt_bias, norm_g, layer, batch, lb_rows, ctx_len, heads, off_q, off_k, off_v):
    t = p.shape[0]
    depth, width, _ = conv_w.shape
    kb_cols = heads * HEAD_DIM
    n_chunks = lb_rows // CHUNK

    def col(off):
        return pl.BlockSpec((lb_rows, HEAD_DIM), lambda b, h: (b, off // HEAD_DIM + h))

    def cw(off):
        return pl.BlockSpec((None, width, HEAD_DIM), lambda b, h: (layer, 0, off // HEAD_DIM + h))

    smem = pl.BlockSpec(memory_space=pltpu.SMEM)
    kern = functools.partial(_dn_kernel, layer=layer, ctx_len=ctx_len)
    row_buf = pltpu.VMEM((lb_rows, HEAD_DIM), f32)
    return pl.pallas_call(
        kern,
        grid=(batch, heads),
        in_specs=[smem, smem, col(off_q), col(off_k), col(off_v),
                  pl.BlockSpec((None, lb_rows, 8), lambda b, h: (h, b, 0)),
                  cw(0), cw(kb_cols), cw(2 * kb_cols),
                  pl.BlockSpec((None, 1, HEAD_DIM), lambda b, h: (layer, 0, 0))],
        out_specs=pl.BlockSpec((lb_rows, HEAD_DIM), lambda b, h: (b, h)),
        out_shape=jax.ShapeDtypeStruct((t, heads * HEAD_DIM), f32),
        scratch_shapes=[row_buf] * 7 + [pltpu.VMEM((lb_rows, CHUNK), f32), pltpu.VMEM((n_chunks, SUB, LANES), f32)],
        compiler_params=_cparams("parallel", "parallel"),
    )(a_log, dt_bias, p, p, p, ab, conv_w, conv_w, conv_w, norm_g.reshape(depth, 1, HEAD_DIM))


def _merge_kernel(s_ref, mod_ref, oa_ref, ob_ref, ag_ref, bz_ref, ga_ref, gb_ref, wa_ref, wb_ref, wo_ref, o_ref,
                  *, tiles_per_batch, ctx_len):
    is_ctx = _is_ctx_rows(pl.program_id(0), tiles_per_batch, s_ref.shape[0], ctx_len)
    ya = _bf(oa_ref[...] * _silu(ag_ref[...]))
    yb = _bf(ob_ref[...] * _silu(bz_ref[...]))
    y = (_sigmoid(ga_ref[...]) * jnp.dot(ya, wa_ref[...], preferred_element_type=f32)
         + _sigmoid(gb_ref[...]) * jnp.dot(yb, wb_ref[...], preferred_element_type=f32))
    mod = mod_ref[...]
    gate = jnp.where(is_ctx, mod[5:6], mod[2:3])
    o_ref[...] = s_ref[...] + gate * _dot(y, wo_ref[...])


def _merge(s, mods, oa, ob, p, wa, wb, wo, layer, lb, ctx_len, off_ag, off_bz):
    t, d = s.shape
    wa_cols = oa.shape[1]
    wb_cols = ob.shape[1]
    tm = _largest_tile(lb, SUB, 256)
    tpb = lb // tm
    kern = functools.partial(_merge_kernel, tiles_per_batch=tpb, ctx_len=ctx_len)

    def whole(a):
        return pl.BlockSpec((None,) + a.shape[1:], lambda i: (layer, 0, 0))

    return pl.pallas_call(
        kern,
        grid=(t // tm,),
        in_specs=[pl.BlockSpec((tm, d), lambda i: (i, 0)),
                  pl.BlockSpec((None, None, None, 8, d), lambda i: (layer, 1, i // tpb, 0, 0)),
                  pl.BlockSpec((tm, wa_cols), lambda i: (i, 0)),
                  pl.BlockSpec((tm, wb_cols), lambda i: (i, 0)),
                  pl.BlockSpec((tm, wa_cols), lambda i: (i, off_ag // wa_cols)),
                  pl.BlockSpec((tm, wb_cols), lambda i: (i, off_bz // wb_cols)),
                  pl.BlockSpec((tm, d), lambda i: (i, 0)),
                  pl.BlockSpec((tm, d), lambda i: (i, 1)),
                  whole(wa), whole(wb), whole(wo)],
        out_specs=pl.BlockSpec((tm, d), lambda i: (i, 0)),
        out_shape=jax.ShapeDtypeStruct((t, d), f32),
        compiler_params=_cparams("parallel"),
    )(s, mods, oa, ob, p, p, p, p, wa, wb, wo)


def _final_kernel(s_ref, g_ref, o_ref):
    s = s_ref[...]
    o_ref[...] = s * lax.rsqrt(jnp.mean(s * s, axis=-1, keepdims=True) + EPS) * g_ref[...]


def _final_norm(s3, g, ctx_len, seq):
    batch, _, d = s3.shape
    tm = _largest_tile(ctx_len, SUB, 256)
    return pl.pallas_call(
        _final_kernel,
        grid=(batch, seq // tm),
        in_specs=[pl.BlockSpec((None, tm, d), lambda b, j: (b, ctx_len // tm + j, 0)),
                  pl.BlockSpec((1, d), lambda b, j: (0, 0))],
        out_specs=pl.BlockSpec((None, tm, d), lambda b, j: (b, j, 0)),
        out_shape=jax.ShapeDtypeStruct((batch, seq, d), f32),
        compiler_params=_cparams("parallel", "parallel"),
    )(s3, g.reshape(1, d))


def kernel(x, c, ctx, c_ctx, w_ada, b_ada, norm_g, final_norm_g, ffn_w_gate, ffn_w_up, ffn_w_down, w_in,
           hgrn_lower_bounds, hgrn_norm_g, dn_conv_w, dn_a_log, dn_dt_bias, dn_norm_g, w_branch_a, w_branch_b, w_out):
    batch, seq, d = x.shape
    ctx_len = ctx.shape[1]
    depth = w_ada.shape[0]
    lb = ctx_len + seq
    ka = hgrn_lower_bounds.shape[-1]
    wa = w_branch_a.shape[1]
    wb = w_branch_b.shape[1]
    kb = (dn_conv_w.shape[-1] - wb) // 2
    heads = dn_a_log.shape[-1]
    assert ka == wa == kb == wb == heads * HEAD_DIM and seq % CHUNK == 0 and ctx_len % CHUNK == 0
    assert d % wa == 0 and 4 * heads <= LANES and batch + 1 <= 8

    cc = jnp.concatenate([c, c_ctx[None], jnp.zeros((8 - batch - 1, d), f32)], axis=0)
    m = _adaln(cc, w_ada, b_ada).reshape(depth, 8, 3, 3, d)
    m_own = jnp.transpose(m[:, :batch], (0, 2, 1, 3, 4))
    m_ctx = jnp.broadcast_to(m[:, batch][:, :, None], (depth, 3, batch, 3, d))
    mods = jnp.concatenate([m_own, m_ctx, jnp.zeros((depth, 3, batch, 2, d), f32)], axis=3)
    norm_g4 = norm_g.reshape(depth, 3, 1, d)

    o_ba = 3 * ka + 2 * wa + 2 * kb + wb + wb
    o_ga = o_ba + 4 * heads
    w_cols = jnp.concatenate([w_in[:, :, o_ga:], w_in[:, :, :o_ba], w_in[:, :, o_ba:o_ga],
                              jnp.zeros((depth, d, LANES - 4 * heads), f32)], axis=2).astype(bf16)
    base = 2 * d
    off_q, off_ff, off_fb, off_v, off_ag = base, base + ka, base + 2 * ka, base + 3 * ka, base + 3 * ka + wa
    off_bq = base + 3 * ka + 2 * wa
    off_bk, off_bv, off_bz = off_bq + kb, off_bq + 2 * kb, off_bq + 2 * kb + wb
    off_ab = base + o_ba

    wg, wu, wd = ffn_w_gate.astype(bf16), ffn_w_up.astype(bf16), ffn_w_down.astype(bf16)
    wbr_a, wbr_b, wo = w_branch_a.astype(bf16), w_branch_b.astype(bf16), w_out.astype(bf16)

    s = jnp.concatenate([ctx, x], axis=1).reshape(batch * lb, d)
    for l in range(depth):
        s = _ffn(s, norm_g4, mods, wg, wu, wd, l, 0, 0, lb, ctx_len)
        p = _proj(s, norm_g4, mods, w_cols, l, lb, ctx_len)
        ab = p[:, off_ab:off_ab + 4 * heads].reshape(batch * lb, 4, heads)
        ab = jnp.pad(jnp.transpose(ab, (2, 0, 1)), ((0, 0), (0, 0), (0, 4)))
        oa = _hgrn(p, hgrn_lower_bounds, hgrn_norm_g, l, batch, lb, ctx_len, heads, off_q, off_ff, off_fb, off_v)
        ob = _dn(p, ab, dn_conv_w, dn_a_log, dn_dt_bias, dn_norm_g, l, batch, lb, ctx_len, heads, off_bq, off_bk, off_bv)
        s = _merge(s, mods, oa, ob, p, wbr_a, wbr_b, wo, l, lb, ctx_len, off_ag, off_bz)
        s = _ffn(s, norm_g4, mods, wg, wu, wd, l, 2, 1, lb, ctx_len)
    return _final_norm(s.reshape(batch, lb, d), final_norm_g, ctx_len, seq)
```

```python
import functools

import jax
import jax.numpy as jnp
from jax import lax
from jax.experimental import pallas as pl
from jax.experimental.pallas import tpu as pltpu

EPS = 1e-6
FFN_RES = 0.5
HEAD_DIM = 128
CHUNK = 64
SUB = 8
LANES = 128
VMEM_LIMIT_BYTES = 56 * 1024 * 1024

f32 = jnp.float32
bf16 = jnp.bfloat16


def _cparams(*sem):
    return pltpu.CompilerParams(dimension_semantics=sem, vmem_limit_bytes=VMEM_LIMIT_BYTES)


def _bf(a):
    return a.astype(bf16)


def _dot(a, b):
    return jnp.dot(_bf(a), _bf(b), preferred_element_type=f32)


def _dot_nt(a, b):
    return lax.dot_general(_bf(a), _bf(b), (((1,), (1,)), ((), ())), preferred_element_type=f32)


def _dot_tn(a, b):
    return lax.dot_general(_bf(a), _bf(b), (((0,), (0,)), ((), ())), preferred_element_type=f32)


def _split2(a):
    hi = _bf(a)
    lo = _bf(a - hi.astype(f32))
    return hi, lo


def _split3(a):
    hi = _bf(a)
    r = a - hi.astype(f32)
    mid = _bf(r)
    lo = _bf(r - mid.astype(f32))
    return hi, mid, lo


def _dot_sel(m01x3, a):
    return jnp.dot(m01x3, jnp.concatenate(_split3(a), axis=0), preferred_element_type=f32)


def _dot_hi(a, b):
    ah, al = _split2(a)
    bh, bl = _split2(b)
    n = b.shape[1]
    r = jnp.dot(ah, jnp.concatenate([bh, bl], axis=1), preferred_element_type=f32)
    return r[:, :n] + r[:, n:] + jnp.dot(al, bh, preferred_element_type=f32)


def _sigmoid(a):
    return 1.0 / (1.0 + jnp.exp(-a))


def _silu(a):
    return a * _sigmoid(a)


def _largest_tile(total, unit, cap):
    n = total // unit
    best = 1
    for k in range(1, n + 1):
        if n % k == 0 and k * unit <= cap:
            best = k
    return best * unit


def _adaln_kernel(c_ref, w_ref, b_ref, o_ref):
    a = _silu(c_ref[...])
    o_ref[...] = _dot(a, w_ref[...]) + b_ref[...]


def _adaln(cc, w_ada, b_ada):
    depth, d, n = w_ada.shape
    rows = cc.shape[0]
    tn = _largest_tile(n, LANES, 1024)
    return pl.pallas_call(
        _adaln_kernel,
        grid=(depth, n // tn),
        in_specs=[pl.BlockSpec((rows, d), lambda l, j: (0, 0)),
                  pl.BlockSpec((None, d, tn), lambda l, j: (l, 0, j)),
                  pl.BlockSpec((None, 1, tn), lambda l, j: (l, 0, j))],
        out_specs=pl.BlockSpec((None, rows, tn), lambda l, j: (l, 0, j)),
        out_shape=jax.ShapeDtypeStruct((depth, rows, n), f32),
        compiler_params=_cparams("parallel", "parallel"),
    )(cc, w_ada, b_ada.reshape(depth, 1, n))


def _is_ctx_rows(tile_idx, tiles_per_batch, tm, ctx_len):
    row0 = (tile_idx % tiles_per_batch) * tm
    rows = row0 + lax.broadcasted_iota(jnp.int32, (tm, 1), 0)
    return rows < ctx_len


def _norm_mod(s, g, mod, is_ctx):
    y = s * lax.rsqrt(jnp.mean(s * s, axis=-1, keepdims=True) + EPS) * g
    shift = jnp.where(is_ctx, mod[3:4], mod[0:1])
    scale = jnp.where(is_ctx, mod[4:5], mod[1:2])
    return y * (1.0 + scale) + shift


def _ffn_kernel(s_ref, g_ref, mod_ref, wg_ref, wu_ref, wd_ref, o_ref, h_ref, *, tiles_per_batch, ctx_len):
    i = pl.program_id(0)
    f = pl.program_id(1)
    tm = s_ref.shape[0]
    is_ctx = _is_ctx_rows(i, tiles_per_batch, tm, ctx_len)

    @pl.when(f == 0)
    def _():
        h_ref[...] = _bf(_norm_mod(s_ref[...], g_ref[...], mod_ref[...], is_ctx))
        o_ref[...] = jnp.zeros_like(o_ref)

    h = h_ref[...]
    a = jnp.dot(h, wg_ref[...], preferred_element_type=f32)
    u = jnp.dot(h, wu_ref[...], preferred_element_type=f32)
    o_ref[...] += _dot(_silu(a) * u, wd_ref[...])

    @pl.when(f == pl.num_programs(1) - 1)
    def _():
        mod = mod_ref[...]
        gate = jnp.where(is_ctx, mod[5:6], mod[2:3])
        o_ref[...] = s_ref[...] + FFN_RES * gate * o_ref[...]


def _ffn(s, g, mods, wg, wu, wd, layer, sub, which, lb, ctx_len):
    t, d = s.shape
    dff = wg.shape[-1]
    tm = _largest_tile(lb, SUB, 576)
    tf = _largest_tile(dff, LANES, 512)
    tpb = lb // tm
    kern = functools.partial(_ffn_kernel, tiles_per_batch=tpb, ctx_len=ctx_len)
    return pl.pallas_call(
        kern,
        grid=(t // tm, dff // tf),
        in_specs=[pl.BlockSpec((tm, d), lambda i, f: (i, 0)),
                  pl.BlockSpec((None, None, 1, d), lambda i, f: (layer, sub, 0, 0)),
                  pl.BlockSpec((None, None, None, 8, d), lambda i, f: (layer, sub, i // tpb, 0, 0)),
                  pl.BlockSpec((None, None, d, tf), lambda i, f: (layer, which, 0, f)),
                  pl.BlockSpec((None, None, d, tf), lambda i, f: (layer, which, 0, f)),
                  pl.BlockSpec((None, None, tf, d), lambda i, f: (layer, which, f, 0))],
        out_specs=pl.BlockSpec((tm, d), lambda i, f: (i, 0)),
        out_shape=jax.ShapeDtypeStruct((t, d), f32),
        scratch_shapes=[pltpu.VMEM((tm, d), bf16)],
        compiler_params=_cparams("parallel", "arbitrary"),
    )(s, g, mods, wg, wu, wd)


def _proj_kernel(s_ref, g_ref, mod_ref, w_ref, o_ref, h_ref, *, tiles_per_batch, ctx_len):
    i = pl.program_id(0)

    @pl.when(pl.program_id(1) == 0)
    def _():
        is_ctx = _is_ctx_rows(i, tiles_per_batch, s_ref.shape[0], ctx_len)
        h_ref[...] = _bf(_norm_mod(s_ref[...], g_ref[...], mod_ref[...], is_ctx))

    o_ref[...] = jnp.dot(h_ref[...], w_ref[...], preferred_element_type=f32)


def _proj(s, g, mods, w, layer, lb, ctx_len):
    t, d = s.shape
    n = w.shape[-1]
    tm = _largest_tile(lb, SUB, 768)
    tn = _largest_tile(n, LANES, 1024)
    tpb = lb // tm
    kern = functools.partial(_proj_kernel, tiles_per_batch=tpb, ctx_len=ctx_len)
    return pl.pallas_call(
        kern,
        grid=(t // tm, n // tn),
        in_specs=[pl.BlockSpec((tm, d), lambda i, j: (i, 0)),
                  pl.BlockSpec((None, None, 1, d), lambda i, j: (layer, 1, 0, 0)),
                  pl.BlockSpec((None, None, None, 8, d), lambda i, j: (layer, 1, i // tpb, 0, 0)),
                  pl.BlockSpec((None, d, tn), lambda i, j: (layer, 0, j))],
        out_specs=pl.BlockSpec((tm, tn), lambda i, j: (i, j)),
        out_shape=jax.ShapeDtypeStruct((t, n), f32),
        scratch_shapes=[pltpu.VMEM((tm, d), bf16)],
        compiler_params=_cparams("parallel", "arbitrary"),
    )(s, g, mods, w)


def _chunk_index(i, n_ctx, n_chunks, reverse):
    if not reverse:
        return i
    return jnp.where(i < n_ctx, n_ctx - 1 - i, n_chunks + n_ctx - 1 - i)


def _group_size(n_chunks, n_ctx):
    for g in (4, 2):
        if n_chunks % g == 0 and n_ctx % g == 0:
            return g
    return 1


def _causal_masks(c, reverse):
    t = lax.broadcasted_iota(jnp.int32, (c, c), 0)
    s = lax.broadcasted_iota(jnp.int32, (c, c), 1)
    if reverse:
        return s >= t, s > t, t, s
    return s <= t, s < t, t, s


def _head_rms(o, g):
    return o * lax.rsqrt(jnp.mean(o * o, axis=-1, keepdims=True) + EPS) * g


def _hgrn_group(items, sts, consts):
    c = items[0]["q"].shape[0]
    sts = list(sts)
    for x in items:
        x["b"] = _dot_sel(consts[x["d"]][0], x["lc"])
    for x in items:
        d, b = x["d"], x["b"]
        b_end = b[0:1] if d else b[c - 1:c]
        x["o"] = _dot_nt(x["q"] * jnp.exp(b), sts[d])
        sts[d] = sts[d] * jnp.exp(b_end) + _dot_tn(x["v"], x["k"] * jnp.exp(b_end - b))

    for x in items:
        x["attn"] = jnp.zeros((c, c), f32)
    m, level = c // 2, 0
    while m >= SUB:
        n = c // m
        for x in items:
            b3 = x["b"].reshape(n, m, LANES)
            if x["d"]:
                own = b3[:, m - 1:m]
                nbr = jnp.concatenate([own[:1], own[:-1]], axis=0)
            else:
                own = b3[:, 0:1]
                nbr = jnp.concatenate([own[1:], own[-1:]], axis=0)
            qs = (x["q"].reshape(n, m, LANES) * jnp.exp(jnp.minimum(b3 - own, 0.0))).reshape(c, LANES)
            ks = (x["k"].reshape(n, m, LANES) * jnp.exp(jnp.minimum(nbr - b3, 0.0))).reshape(c, LANES)
            x["lv"] = _dot_nt(qs, ks)
        for x in items:
            x["attn"] = x["attn"] + jnp.where(consts[x["d"]][1][level], x["lv"], 0.0)
        m, level = m // 2, level + 1
    for x in items:
        x["ov"] = _dot(x["attn"], x["v"])

    n = c // SUB
    for x in items:
        sub_mask, ones_red = consts[x["d"]][2:]
        b3 = x["b"].reshape(n, SUB, LANES)
        q3 = x["q"].reshape(n, SUB, LANES)
        k3 = x["k"].reshape(n, SUB, LANES)
        prods = []
        for j in range(SUB):
            e = jnp.exp(jnp.minimum(b3 - b3[:, j:j + 1], 0.0))
            p = jnp.where(sub_mask[j], q3 * e * k3[:, j:j + 1], 0.0)
            prods.append(p.reshape(c, LANES))
        x["sums"] = jnp.dot(_bf(jnp.concatenate(prods, axis=0)), ones_red, preferred_element_type=f32)
    for x in items:
        v3 = x["v"].reshape(n, SUB, LANES)
        od = jnp.zeros((n, SUB, LANES), f32)
        for j in range(SUB):
            od = od + x["sums"][j * c:(j + 1) * c].reshape(n, SUB, LANES) * v3[:, j:j + 1]
        x["o"] = x["o"] + x["ov"] + od.reshape(c, LANES)
    return sts


def _hgrn_consts(c, reverse):
    m_incl, _, t, s = _causal_masks(c, reverse)
    level_masks = []
    m = c // 2
    while m >= SUB:
        tb, sb = t // m, s // m
        if reverse:
            level_masks.append((tb % 2 == 0) & (sb == tb + 1))
        else:
            level_masks.append((tb % 2 == 1) & (sb == tb - 1))
        m //= 2
    r = lax.broadcasted_iota(jnp.int32, (1, SUB, LANES), 1)
    sub_mask = [(r <= j) if reverse else (r >= j) for j in range(SUB)]
    row3 = lax.broadcasted_iota(jnp.int32, (c, 3 * c), 0)
    lane3 = lax.broadcasted_iota(jnp.int32, (c, 3 * c), 1) % c
    sel3 = jnp.where((lane3 >= row3) if reverse else (lane3 <= row3), 1.0, 0.0).astype(bf16)
    return (sel3, level_masks, sub_mask, jnp.ones((LANES, LANES), bf16))


def _hgrn_kernel(q_ref, ff_ref, fb_ref, v_ref, lb_ref, g_ref, o_ref, *, layer, ctx_len):
    c = CHUNK
    n_chunks = q_ref.shape[0] // c
    n_ctx = ctx_len // c

    raw = lb_ref[...]
    e = jnp.exp(raw - jnp.max(raw, axis=0, keepdims=True))
    sm = e / jnp.sum(e, axis=0, keepdims=True)
    lbs = jnp.sum(sm[:layer + 1], axis=0) - sm[0]

    consts = [_hgrn_consts(c, False), _hgrn_consts(c, True)]
    grp = _group_size(n_chunks, n_ctx)
    o_ref[...] = jnp.zeros_like(o_ref)

    def body(it, sts):
        items = []
        for j in range(grp):
            for d in (0, 1):
                r0 = pl.multiple_of(_chunk_index(it * grp + j, n_ctx, n_chunks, d == 1) * c, c)
                xf = (fb_ref if d else ff_ref)[pl.ds(r0, c), :]
                lb = lbs[d:d + 1]
                items.append(dict(d=d, r0=r0, q=_silu(q_ref[pl.ds(r0, c), :]), v=v_ref[pl.ds(r0, c), :],
                                  lc=jnp.log(lb + (1.0 - lb) * _sigmoid(xf)), k=(1.0 - lb) * _sigmoid(-xf)))
        sts = _hgrn_group(items, sts, consts)
        for x in items:
            o_ref[pl.ds(x["r0"], c), :] += x["o"]
        return tuple(sts)

    zero_state = jnp.zeros((HEAD_DIM, HEAD_DIM), f32)
    lax.fori_loop(0, n_chunks // grp, body, (zero_state, zero_state))

    o_ref[...] = _head_rms(o_ref[...], g_ref[...])


def _hgrn(p, lower_bounds, norm_g, layer, batch, lb_rows, ctx_len, heads, off_q, off_ff, off_fb, off_v):
    t = p.shape[0]
    depth = lower_bounds.shape[0]

    def col(off):
        return pl.BlockSpec((lb_rows, HEAD_DIM), lambda b, h: (b, off // HEAD_DIM + h))

    kern = functools.partial(_hgrn_kernel, layer=layer, ctx_len=ctx_len)
    return pl.pallas_call(
        kern,
        grid=(batch, heads),
        in_specs=[col(off_q), col(off_ff), col(off_fb), col(off_v),
                  pl.BlockSpec((depth, 2, HEAD_DIM), lambda b, h: (0, 0, h)),
                  pl.BlockSpec((None, 1, HEAD_DIM), lambda b, h: (layer, 0, 0))],
        out_specs=pl.BlockSpec((lb_rows, HEAD_DIM), lambda b, h: (b, h)),
        out_shape=jax.ShapeDtypeStruct((t, heads * HEAD_DIM), f32),
        compiler_params=_cparams("parallel", "parallel"),
    )(p, p, p, p, lower_bounds, norm_g.reshape(depth, 1, HEAD_DIM))


def _short_conv_tile(u, cw, width):
    r = u.shape[0]
    rows = lax.broadcasted_iota(jnp.int32, (r, 1), 0)
    half = width // 2
    acc = u * cw[half:half + 1]
    for j in range(width):
        if j == half:
            continue
        uj = pltpu.roll(u, (half - j) % r, 0)
        valid = rows >= (half - j) if j < half else rows < r - (j - half)
        acc = acc + jnp.where(valid, uj, 0.0) * cw[j:j + 1]
    return acc


def _dn_kernel(alog_ref, dt_ref, q_ref, k_ref, v_ref, ab_ref, cwq_ref, cwk_ref, cwv_ref, g_ref, o_ref,
               qn, kn, vn, u_s, w_s, qg_s, kd_s, qk_s, gl_s, *, layer, ctx_len):
    c = CHUNK
    h = pl.program_id(1)
    n_chunks = q_ref.shape[0] // c
    n_ctx = ctx_len // c
    width = cwq_ref.shape[0]

    def prep(r0, rows):
        for src, cw, dst, mode in ((q_ref, cwq_ref, qn, "q"), (k_ref, cwk_ref, kn, "k"), (v_ref, cwv_ref, vn, "v")):
            y = _silu(_short_conv_tile(src[pl.ds(r0, rows), :], cw[...], width))
            if mode != "v":
                y = y * lax.rsqrt(jnp.sum(y * y, axis=-1, keepdims=True) + EPS)
            if mode == "q":
                y = y * (HEAD_DIM ** -0.5)
            dst[pl.ds(r0, rows), :] = y

    prep(0, ctx_len)

    def prep_body(i, carry):
        prep(pl.multiple_of(ctx_len + i * c, c), c)
        return carry

    lax.fori_loop(0, n_chunks - n_ctx, prep_body, 0)

    row = lax.broadcasted_iota(jnp.int32, (c, LANES), 0)
    lane = lax.broadcasted_iota(jnp.int32, (c, LANES), 1)
    left_lanes = lane < c
    eye_right = jnp.where(lane == row + c, 1.0, 0.0)
    row3 = lax.broadcasted_iota(jnp.int32, (c, 3 * c), 0)
    lane3 = lax.broadcasted_iota(jnp.int32, (c, 3 * c), 1) % c
    ones3 = jnp.ones((c, 3 * c), bf16)
    zeros_k = jnp.zeros((c, HEAD_DIM), f32)
    dirs = []
    for d in (0, 1):
        if d == 0:
            m_incl, m_strict, m_incl_t, sel3 = lane <= row, lane < row, (row <= lane) & left_lanes, lane3 <= row3
        else:
            m_incl, m_strict, m_incl_t, sel3 = (lane >= row) & left_lanes, (lane > row) & left_lanes, lane <= row, lane3 >= row3
        neg_a = -jnp.exp(jnp.full((1, 1), alog_ref[layer, d, h], f32))
        dirs.append((m_incl, m_strict, m_incl_t, jnp.where(sel3, 1.0, 0.0).astype(bf16), neg_a, dt_ref[layer, d, h]))

    grp = _group_size(n_chunks, n_ctx)

    def pre_body(it, carry):
        chains = []
        for j in range(grp):
            ci = it * grp + j
            r0 = pl.multiple_of(ci * c, c)
            q = qn[pl.ds(r0, c), :]
            k = kn[pl.ds(r0, c), :]
            kq = _dot_nt(jnp.concatenate([k, q], axis=0), jnp.concatenate([k, zeros_k], axis=0))
            for d in (0, 1):
                chains.append(dict(d=d, ci=ci, r0=r0, q=q, k=k, kq=kq))
        for x in chains:
            d = x["d"]
            _, _, m_incl_t, sel3, neg_a, dt = dirs[d]
            ab = ab_ref[pl.ds(x["r0"], c), :]
            za = ab[:, d:d + 1] + dt
            g = neg_a * (jnp.maximum(za, 0.0) + jnp.log(1.0 + jnp.exp(-jnp.abs(za))))
            gfull = jnp.broadcast_to(g, (c, LANES))
            x["beta"] = _sigmoid(ab[:, 2 + d:3 + d])
            x["gc"] = _dot_sel(sel3, gfull)
            x["gc_row"] = _dot_sel(ones3, jnp.where(m_incl_t, gfull, 0.0))
        for x in chains:
            m_incl, m_strict = dirs[x["d"]][:2]
            x["gamma"] = jnp.where(m_incl, jnp.exp(jnp.minimum(x["gc"] - x["gc_row"], 0.0)), 0.0)
            x["x"] = eye_right - jnp.where(m_strict, x["beta"] * x["kq"][:c] * x["gamma"], 0.0)
        span = 1
        while span < c:
            for x in chains:
                x["r"] = _dot_hi(x["x"][:, :c], x["x"])
            for x in chains:
                x["x"] = x["r"] + jnp.where(left_lanes, 0.0, x["x"])
            span *= 2
        for x in chains:
            r0 = x["r0"]
            x["egc"] = jnp.exp(x["gc"])
            kb = x["k"] * x["beta"]
            v = vn[pl.ds(r0, c), :]
            x["sol"] = _dot_hi(x["x"][:, c:], jnp.concatenate([v * x["beta"], kb * x["egc"]], axis=1))
        for x in chains:
            d, r0, gc = x["d"], x["r0"], x["gc"]
            gc_end = gc[0:1] if d else gc[c - 1:c]
            u_s[d, pl.ds(r0, c), :] = x["sol"][:, :HEAD_DIM]
            w_s[d, pl.ds(r0, c), :] = x["sol"][:, HEAD_DIM:]
            qg_s[d, pl.ds(r0, c), :] = x["q"] * x["egc"]
            kd_s[d, pl.ds(r0, c), :] = x["k"] * jnp.exp(gc_end - gc)
            qk_s[d, pl.ds(r0, c), :] = x["kq"][c:] * x["gamma"]
            gl_s[d, x["ci"]] = jnp.broadcast_to(gc_end, (SUB, LANES))
        return carry

    lax.fori_loop(0, n_chunks // grp, pre_body, 0)

    o_ref[...] = jnp.zeros_like(o_ref)

    def scan_body(i, sts):
        cis = [_chunk_index(i, n_ctx, n_chunks, d == 1) for d in (0, 1)]
        r0s = [pl.multiple_of(ci * c, c) for ci in cis]
        ws = [_dot(jnp.concatenate([w_s[d, pl.ds(r0s[d], c), :], qg_s[d, pl.ds(r0s[d], c), :]], axis=0), sts[d])
              for d in (0, 1)]
        v_new = [u_s[d, pl.ds(r0s[d], c), :] - ws[d][:c] for d in (0, 1)]
        upd = [_dot_tn(kd_s[d, pl.ds(r0s[d], c), :], v_new[d]) for d in (0, 1)]
        intra = [_dot(qk_s[d, pl.ds(r0s[d], c), :][:, :c], v_new[d]) for d in (0, 1)]
        for d in (0, 1):
            o_ref[pl.ds(r0s[d], c), :] += ws[d][c:] + intra[d]
        return tuple(jnp.exp(gl_s[d, cis[d]][0:1]) * sts[d] + upd[d] for d in (0, 1))

    zero_state = jnp.zeros((HEAD_DIM, HEAD_DIM), f32)
    lax.fori_loop(0, n_chunks, scan_body, (zero_state, zero_state))

    o_ref[...] = _head_rms(o_ref[...], g_ref[...])


def _dn(p, ab, conv_w, a_log, dt_bias, norm_g, layer, batch, lb_rows, ctx_len, heads, off_q, off_k, off_v):
    t = p.shape[0]
    depth, width, _ = conv_w.shape
    kb_cols = heads * HEAD_DIM
    n_chunks = lb_rows // CHUNK

    def col(off):
        return pl.BlockSpec((lb_rows, HEAD_DIM), lambda b, h: (b, off // HEAD_DIM + h))

    def cw(off):
        return pl.BlockSpec((None, width, HEAD_DIM), lambda b, h: (layer, 0, off // HEAD_DIM + h))

    smem = pl.BlockSpec(memory_space=pltpu.SMEM)
    kern = functools.partial(_dn_kernel, layer=layer, ctx_len=ctx_len)
    row_buf = pltpu.VMEM((lb_rows, HEAD_DIM), f32)
    dir_buf = pltpu.VMEM((2, lb_rows, HEAD_DIM), f32)
    return pl.pallas_call(
        kern,
        grid=(batch, heads),
        in_specs=[smem, smem, col(off_q), col(off_k), col(off_v),
                  pl.BlockSpec((None, lb_rows, 8), lambda b, h: (h, b, 0)),
                  cw(0), cw(kb_cols), cw(2 * kb_cols),
                  pl.BlockSpec((None, 1, HEAD_DIM), lambda b, h: (layer, 0, 0))],
        out_specs=pl.BlockSpec((lb_rows, HEAD_DIM), lambda b, h: (b, h)),
        out_shape=jax.ShapeDtypeStruct((t, heads * HEAD_DIM), f32),
        scratch_shapes=[row_buf] * 3 + [dir_buf] * 5 + [pltpu.VMEM((2, n_chunks, SUB, LANES), f32)],
        compiler_params=_cparams("parallel", "parallel"),
    )(a_log, dt_bias, p, p, p, ab, conv_w, conv_w, conv_w, norm_g.reshape(depth, 1, HEAD_DIM))


def _merge_kernel(s_ref, mod_ref, oa_ref, ob_ref, ag_ref, bz_ref, ga_ref, gb_ref, wa_ref, wb_ref, wo_ref, o_ref,
                  *, tiles_per_batch, ctx_len):
    is_ctx = _is_ctx_rows(pl.program_id(0), tiles_per_batch, s_ref.shape[0], ctx_len)
    ya = _bf(oa_ref[...] * _silu(ag_ref[...]))
    yb = _bf(ob_ref[...] * _silu(bz_ref[...]))
    y = (_sigmoid(ga_ref[...]) * jnp.dot(ya, wa_ref[...], preferred_element_type=f32)
         + _sigmoid(gb_ref[...]) * jnp.dot(yb, wb_ref[...], preferred_element_type=f32))
    mod = mod_ref[...]
    gate = jnp.where(is_ctx, mod[5:6], mod[2:3])
    o_ref[...] = s_ref[...] + gate * _dot(y, wo_ref[...])


def _merge(s, mods, oa, ob, p, wa, wb, wo, layer, lb, ctx_len, off_ag, off_bz):
    t, d = s.shape
    wa_cols = oa.shape[1]
    wb_cols = ob.shape[1]
    tm = _largest_tile(lb, SUB, 256)
    tpb = lb // tm
    kern = functools.partial(_merge_kernel, tiles_per_batch=tpb, ctx_len=ctx_len)

    def whole(a):
        return pl.BlockSpec((None,) + a.shape[1:], lambda i: (layer, 0, 0))

    return pl.pallas_call(
        kern,
        grid=(t // tm,),
        in_specs=[pl.BlockSpec((tm, d), lambda i: (i, 0)),
                  pl.BlockSpec((None, None, None, 8, d), lambda i: (layer, 1, i // tpb, 0, 0)),
                  pl.BlockSpec((tm, wa_cols), lambda i: (i, 0)),
                  pl.BlockSpec((tm, wb_cols), lambda i: (i, 0)),
                  pl.BlockSpec((tm, wa_cols), lambda i: (i, off_ag // wa_cols)),
                  pl.BlockSpec((tm, wb_cols), lambda i: (i, off_bz // wb_cols)),
                  pl.BlockSpec((tm, d), lambda i: (i, 0)),
                  pl.BlockSpec((tm, d), lambda i: (i, 1)),
                  whole(wa), whole(wb), whole(wo)],
        out_specs=pl.BlockSpec((tm, d), lambda i: (i, 0)),
        out_shape=jax.ShapeDtypeStruct((t, d), f32),
        compiler_params=_cparams("parallel"),
    )(s, mods, oa, ob, p, p, p, p, wa, wb, wo)


def _final_kernel(s_ref, g_ref, o_ref):
    s = s_ref[...]
    o_ref[...] = s * lax.rsqrt(jnp.mean(s * s, axis=-1, keepdims=True) + EPS) * g_ref[...]


def _final_norm(s3, g, ctx_len, seq):
    batch, _, d = s3.shape
    tm = _largest_tile(ctx_len, SUB, 256)
    return pl.pallas_call(
        _final_kernel,
        grid=(batch, seq // tm),
        in_specs=[pl.BlockSpec((None, tm, d), lambda b, j: (b, ctx_len // tm + j, 0)),
                  pl.BlockSpec((1, d), lambda b, j: (0, 0))],
        out_specs=pl.BlockSpec((None, tm, d), lambda b, j: (b, j, 0)),
        out_shape=jax.ShapeDtypeStruct((batch, seq, d), f32),
        compiler_params=_cparams("parallel", "parallel"),
    )(s3, g.reshape(1, d))


def kernel(x, c, ctx, c_ctx, w_ada, b_ada, norm_g, final_norm_g, ffn_w_gate, ffn_w_up, ffn_w_down, w_in,
           hgrn_lower_bounds, hgrn_norm_g, dn_conv_w, dn_a_log, dn_dt_bias, dn_norm_g, w_branch_a, w_branch_b, w_out):
    batch, seq, d = x.shape
    ctx_len = ctx.shape[1]
    depth = w_ada.shape[0]
    lb = ctx_len + seq
    ka = hgrn_lower_bounds.shape[-1]
    wa = w_branch_a.shape[1]
    wb = w_branch_b.shape[1]
    kb = (dn_conv_w.shape[-1] - wb) // 2
    heads = dn_a_log.shape[-1]
    assert ka == wa == kb == wb == heads * HEAD_DIM and seq % CHUNK == 0 and ctx_len % CHUNK == 0
    assert d % wa == 0 and 4 * heads <= LANES and batch + 1 <= 8

    cc = jnp.concatenate([c, c_ctx[None], jnp.zeros((8 - batch - 1, d), f32)], axis=0)
    m = _adaln(cc, w_ada, b_ada).reshape(depth, 8, 3, 3, d)
    m_own = jnp.transpose(m[:, :batch], (0, 2, 1, 3, 4))
    m_ctx = jnp.broadcast_to(m[:, batch][:, :, None], (depth, 3, batch, 3, d))
    mods = jnp.concatenate([m_own, m_ctx, jnp.zeros((depth, 3, batch, 2, d), f32)], axis=3)
    norm_g4 = norm_g.reshape(depth, 3, 1, d)

    o_ba = 3 * ka + 2 * wa + 2 * kb + wb + wb
    o_ga = o_ba + 4 * heads
    w_cols = jnp.concatenate([w_in[:, :, o_ga:], w_in[:, :, :o_ba], w_in[:, :, o_ba:o_ga],
                              jnp.zeros((depth, d, LANES - 4 * heads), f32)], axis=2).astype(bf16)
    base = 2 * d
    off_q, off_ff, off_fb, off_v, off_ag = base, base + ka, base + 2 * ka, base + 3 * ka, base + 3 * ka + wa
    off_bq = base + 3 * ka + 2 * wa
    off_bk, off_bv, off_bz = off_bq + kb, off_bq + 2 * kb, off_bq + 2 * kb + wb
    off_ab = base + o_ba

    wg, wu, wd = ffn_w_gate.astype(bf16), ffn_w_up.astype(bf16), ffn_w_down.astype(bf16)
    wbr_a, wbr_b, wo = w_branch_a.astype(bf16), w_branch_b.astype(bf16), w_out.astype(bf16)

    s = jnp.concatenate([ctx, x], axis=1).reshape(batch * lb, d)
    for l in range(depth):
        s = _ffn(s, norm_g4, mods, wg, wu, wd, l, 0, 0, lb, ctx_len)
        p = _proj(s, norm_g4, mods, w_cols, l, lb, ctx_len)
        ab = p[:, off_ab:off_ab + 4 * heads].reshape(batch * lb, 4, heads)
        ab = jnp.pad(jnp.transpose(ab, (2, 0, 1)), ((0, 0), (0, 0), (0, 4)))
        oa = _hgrn(p, hgrn_lower_bounds, hgrn_norm_g, l, batch, lb, ctx_len, heads, off_q, off_ff, off_fb, off_v)
        ob = _dn(p, ab, dn_conv_w, dn_a_log, dn_dt_bias, dn_norm_g, l, batch, lb, ctx_len, heads, off_bq, off_bk, off_bv)
        s = _merge(s, mods, oa, ob, p, wbr_a, wbr_b, wo, l, lb, ctx_len, off_ag, off_bz)
        s = _ffn(s, norm_g4, mods, wg, wu, wd, l, 2, 1, lb, ctx_len)
    return _final_norm(s.reshape(batch, lb, d), final_norm_g, ctx_len, seq)
```

```python
import functools

import jax
import jax.numpy as jnp
from jax import lax
from jax.experimental import pallas as pl
from jax.experimental.pallas import tpu as pltpu

EPS = 1e-6
FFN_RES = 0.5
HEAD_DIM = 128
CHUNK = 64
SUB = 8
LANES = 128
VMEM_LIMIT_BYTES = 56 * 1024 * 1024

f32 = jnp.float32
bf16 = jnp.bfloat16


def _cparams(*sem):
    return pltpu.CompilerParams(dimension_semantics=sem, vmem_limit_bytes=VMEM_LIMIT_BYTES)


def _bf(a):
    return a.astype(bf16)


def _dot(a, b):
    return jnp.dot(_bf(a), _bf(b), preferred_element_type=f32)


def _dot_nt(a, b):
    return lax.dot_general(_bf(a), _bf(b), (((1,), (1,)), ((), ())), preferred_element_type=f32)


def _dot_tn(a, b):
    return lax.dot_general(_bf(a), _bf(b), (((0,), (0,)), ((), ())), preferred_element_type=f32)


def _split2(a):
    hi = _bf(a)
    lo = _bf(a - hi.astype(f32))
    return hi, lo


def _split3(a):
    hi = _bf(a)
    r = a - hi.astype(f32)
    mid = _bf(r)
    lo = _bf(r - mid.astype(f32))
    return hi, mid, lo


def _dot_sel(m01x3, a):
    return jnp.dot(m01x3, jnp.concatenate(_split3(a), axis=0), preferred_element_type=f32)


def _dot_hi(a, b):
    ah, al = _split2(a)
    bh, bl = _split2(b)
    n = b.shape[1]
    r = jnp.dot(ah, jnp.concatenate([bh, bl], axis=1), preferred_element_type=f32)
    return r[:, :n] + r[:, n:] + jnp.dot(al, bh, preferred_element_type=f32)


def _sigmoid(a):
    return 1.0 / (1.0 + jnp.exp(-a))


def _silu(a):
    return a * _sigmoid(a)


def _largest_tile(total, unit, cap):
    n = total // unit
    best = 1
    for k in range(1, n + 1):
        if n % k == 0 and k * unit <= cap:
            best = k
    return best * unit


def _adaln_kernel(c_ref, w_ref, b_ref, o_ref):
    a = _silu(c_ref[...])
    o_ref[...] = _dot(a, w_ref[...]) + b_ref[...]


def _adaln(cc, w_ada, b_ada):
    depth, d, n = w_ada.shape
    rows = cc.shape[0]
    tn = _largest_tile(n, LANES, 1024)
    return pl.pallas_call(
        _adaln_kernel,
        grid=(depth, n // tn),
        in_specs=[pl.BlockSpec((rows, d), lambda l, j: (0, 0)),
                  pl.BlockSpec((None, d, tn), lambda l, j: (l, 0, j)),
                  pl.BlockSpec((None, 1, tn), lambda l, j: (l, 0, j))],
        out_specs=pl.BlockSpec((None, rows, tn), lambda l, j: (l, 0, j)),
        out_shape=jax.ShapeDtypeStruct((depth, rows, n), f32),
        compiler_params=_cparams("parallel", "parallel"),
    )(cc, w_ada, b_ada.reshape(depth, 1, n))


def _is_ctx_rows(tile_idx, tiles_per_batch, tm, ctx_len):
    row0 = (tile_idx % tiles_per_batch) * tm
    rows = row0 + lax.broadcasted_iota(jnp.int32, (tm, 1), 0)
    return rows < ctx_len


def _norm_mod(s, g, mod, is_ctx):
    y = s * lax.rsqrt(jnp.mean(s * s, axis=-1, keepdims=True) + EPS) * g
    shift = jnp.where(is_ctx, mod[3:4], mod[0:1])
    scale = jnp.where(is_ctx, mod[4:5], mod[1:2])
    return y * (1.0 + scale) + shift


def _ffn_kernel(s_ref, g_ref, mod_ref, wg_ref, wu_ref, wd_ref, o_ref, h_ref, *, tiles_per_batch, ctx_len):
    i = pl.program_id(0)
    f = pl.program_id(1)
    tm = s_ref.shape[0]
    is_ctx = _is_ctx_rows(i, tiles_per_batch, tm, ctx_len)

    @pl.when(f == 0)
    def _():
        h_ref[...] = _bf(_norm_mod(s_ref[...], g_ref[...], mod_ref[...], is_ctx))
        o_ref[...] = jnp.zeros_like(o_ref)

    h = h_ref[...]
    a = jnp.dot(h, wg_ref[...], preferred_element_type=f32)
    u = jnp.dot(h, wu_ref[...], preferred_element_type=f32)
    o_ref[...] += _dot(_silu(a) * u, wd_ref[...])

    @pl.when(f == pl.num_programs(1) - 1)
    def _():
        mod = mod_ref[...]
        gate = jnp.where(is_ctx, mod[5:6], mod[2:3])
        o_ref[...] = s_ref[...] + FFN_RES * gate * o_ref[...]


def _ffn(s, g, mods, wg, wu, wd, layer, sub, which, lb, ctx_len):
    t, d = s.shape
    dff = wg.shape[-1]
    tm = _largest_tile(lb, SUB, 576)
    tf = _largest_tile(dff, LANES, 512)
    tpb = lb // tm
    kern = functools.partial(_ffn_kernel, tiles_per_batch=tpb, ctx_len=ctx_len)
    return pl.pallas_call(
        kern,
        grid=(t // tm, dff // tf),
        in_specs=[pl.BlockSpec((tm, d), lambda i, f: (i, 0)),
                  pl.BlockSpec((None, None, 1, d), lambda i, f: (layer, sub, 0, 0)),
                  pl.BlockSpec((None, None, None, 8, d), lambda i, f: (layer, sub, i // tpb, 0, 0)),
                  pl.BlockSpec((None, None, d, tf), lambda i, f: (layer, which, 0, f)),
                  pl.BlockSpec((None, None, d, tf), lambda i, f: (layer, which, 0, f)),
                  pl.BlockSpec((None, None, tf, d), lambda i, f: (layer, which, f, 0))],
        out_specs=pl.BlockSpec((tm, d), lambda i, f: (i, 0)),
        out_shape=jax.ShapeDtypeStruct((t, d), f32),
        scratch_shapes=[pltpu.VMEM((tm, d), bf16)],
        compiler_params=_cparams("parallel", "arbitrary"),
    )(s, g, mods, wg, wu, wd)


def _proj_kernel(s_ref, g_ref, mod_ref, w_ref, o_ref, h_ref, *, tiles_per_batch, ctx_len):
    i = pl.program_id(0)

    @pl.when(pl.program_id(1) == 0)
    def _():
        is_ctx = _is_ctx_rows(i, tiles_per_batch, s_ref.shape[0], ctx_len)
        h_ref[...] = _bf(_norm_mod(s_ref[...], g_ref[...], mod_ref[...], is_ctx))

    o_ref[...] = jnp.dot(h_ref[...], w_ref[...], preferred_element_type=f32)


def _proj(s, g, mods, w, layer, lb, ctx_len):
    t, d = s.shape
    n = w.shape[-1]
    tm = _largest_tile(lb, SUB, 576)
    tn = _largest_tile(n, LANES, 2048)
    tpb = lb // tm
    kern = functools.partial(_proj_kernel, tiles_per_batch=tpb, ctx_len=ctx_len)
    return pl.pallas_call(
        kern,
        grid=(t // tm, n // tn),
        in_specs=[pl.BlockSpec((tm, d), lambda i, j: (i, 0)),
                  pl.BlockSpec((None, None, 1, d), lambda i, j: (layer, 1, 0, 0)),
                  pl.BlockSpec((None, None, None, 8, d), lambda i, j: (layer, 1, i // tpb, 0, 0)),
                  pl.BlockSpec((None, d, tn), lambda i, j: (layer, 0, j))],
        out_specs=pl.BlockSpec((tm, tn), lambda i, j: (i, j)),
        out_shape=jax.ShapeDtypeStruct((t, n), f32),
        scratch_shapes=[pltpu.VMEM((tm, d), bf16)],
        compiler_params=_cparams("parallel", "arbitrary"),
    )(s, g, mods, w)


def _chunk_index(i, n_ctx, n_chunks, reverse):
    if not reverse:
        return i
    return jnp.where(i < n_ctx, n_ctx - 1 - i, n_chunks + n_ctx - 1 - i)


def _group_size(n_chunks, n_ctx):
    for g in (4, 2):
        if n_chunks % g == 0 and n_ctx % g == 0:
            return g
    return 1


def _causal_masks(c, reverse):
    t = lax.broadcasted_iota(jnp.int32, (c, c), 0)
    s = lax.broadcasted_iota(jnp.int32, (c, c), 1)
    if reverse:
        return s >= t, s > t, t, s
    return s <= t, s < t, t, s


def _head_rms(o, g):
    return o * lax.rsqrt(jnp.mean(o * o, axis=-1, keepdims=True) + EPS) * g


def _hgrn_group(items, sts, consts):
    c = items[0]["q"].shape[0]
    sts = list(sts)
    for x in items:
        x["b"] = _dot_sel(consts[x["d"]][0], x["lc"])
    for x in items:
        d, b = x["d"], x["b"]
        b_end = b[0:1] if d else b[c - 1:c]
        x["o"] = _dot_nt(x["q"] * jnp.exp(b), sts[d])
        sts[d] = sts[d] * jnp.exp(b_end) + _dot_tn(x["v"], x["k"] * jnp.exp(b_end - b))

    for x in items:
        x["attn"] = jnp.zeros((c, c), f32)
    m, level = c // 2, 0
    while m >= SUB:
        n = c // m
        zeros_blk = jnp.zeros((m, LANES), f32)
        for x in items:
            d, b, q, k = x["d"], x["b"], x["q"], x["k"]
            qs, ks = [], []
            for i in range(n):
                lo, hi = i * m, (i + 1) * m
                if (i % 2 == 1) != bool(d):
                    edge = b[hi - 1:hi] if d else b[lo:lo + 1]
                    qs.append(q[lo:hi] * jnp.exp(jnp.minimum(b[lo:hi] - edge, 0.0)))
                    ks.append(zeros_blk)
                else:
                    edge = b[lo - 1:lo] if d else b[hi:hi + 1]
                    ks.append(k[lo:hi] * jnp.exp(jnp.minimum(edge - b[lo:hi], 0.0)))
                    qs.append(zeros_blk)
            x["lv"] = _dot_nt(jnp.concatenate(qs, axis=0), jnp.concatenate(ks, axis=0))
        for x in items:
            x["attn"] = x["attn"] + jnp.where(consts[x["d"]][1][level], x["lv"], 0.0)
        m, level = m // 2, level + 1
    for x in items:
        x["ov"] = _dot(x["attn"], x["v"])

    n = c // SUB
    for x in items:
        sub_mask, ones_red = consts[x["d"]][2:]
        b3 = x["b"].reshape(n, SUB, LANES)
        q3 = x["q"].reshape(n, SUB, LANES)
        k3 = x["k"].reshape(n, SUB, LANES)
        prods = [(q3 * k3).reshape(c, LANES)]
        for j in range(1, SUB):
            shift = SUB - j if x["d"] else j
            e = jnp.exp(b3 - pltpu.roll(b3, shift, 1))
            p = jnp.where(sub_mask[j], q3 * e * pltpu.roll(k3, shift, 1), 0.0)
            prods.append(p.reshape(c, LANES))
        x["sums"] = jnp.dot(_bf(jnp.concatenate(prods, axis=0)), ones_red, preferred_element_type=f32)
    for x in items:
        v3 = x["v"].reshape(n, SUB, LANES)
        od = x["sums"][:c].reshape(n, SUB, LANES) * v3
        for j in range(1, SUB):
            shift = SUB - j if x["d"] else j
            od = od + x["sums"][j * c:(j + 1) * c].reshape(n, SUB, LANES) * pltpu.roll(v3, shift, 1)
        x["o"] = x["o"] + x["ov"] + od.reshape(c, LANES)
    return sts


def _hgrn_consts(c, reverse):
    m_incl, _, t, s = _causal_masks(c, reverse)
    level_masks = []
    m = c // 2
    while m >= SUB:
        tb, sb = t // m, s // m
        if reverse:
            level_masks.append((tb % 2 == 0) & (sb == tb + 1))
        else:
            level_masks.append((tb % 2 == 1) & (sb == tb - 1))
        m //= 2
    r = lax.broadcasted_iota(jnp.int32, (1, SUB, LANES), 1)
    sub_mask = [(r <= SUB - 1 - j) if reverse else (r >= j) for j in range(SUB)]
    row3 = lax.broadcasted_iota(jnp.int32, (c, 3 * c), 0)
    lane3 = lax.broadcasted_iota(jnp.int32, (c, 3 * c), 1) % c
    sel3 = jnp.where((lane3 >= row3) if reverse else (lane3 <= row3), 1.0, 0.0).astype(bf16)
    return (sel3, level_masks, sub_mask, jnp.ones((LANES, LANES), bf16))


def _hgrn_kernel(q_ref, ff_ref, fb_ref, v_ref, lb_ref, g_ref, o_ref, *, layer, ctx_len):
    c = CHUNK
    n_chunks = q_ref.shape[0] // c
    n_ctx = ctx_len // c

    raw = lb_ref[...]
    e = jnp.exp(raw - jnp.max(raw, axis=0, keepdims=True))
    sm = e / jnp.sum(e, axis=0, keepdims=True)
    lbs = jnp.sum(sm[:layer + 1], axis=0) - sm[0]

    consts = [_hgrn_consts(c, False), _hgrn_consts(c, True)]
    grp = _group_size(n_chunks, n_ctx)
    o_ref[...] = jnp.zeros_like(o_ref)

    def body(it, sts):
        items = []
        for j in range(grp):
            for d in (0, 1):
                r0 = pl.multiple_of(_chunk_index(it * grp + j, n_ctx, n_chunks, d == 1) * c, c)
                xf = (fb_ref if d else ff_ref)[pl.ds(r0, c), :]
                lb = lbs[d:d + 1]
                items.append(dict(d=d, r0=r0, q=_silu(q_ref[pl.ds(r0, c), :]), v=v_ref[pl.ds(r0, c), :],
                                  lc=jnp.log(lb + (1.0 - lb) * _sigmoid(xf)), k=(1.0 - lb) * _sigmoid(-xf)))
        sts = _hgrn_group(items, sts, consts)
        for x in items:
            o_ref[pl.ds(x["r0"], c), :] += x["o"]
        return tuple(sts)

    zero_state = jnp.zeros((HEAD_DIM, HEAD_DIM), f32)
    lax.fori_loop(0, n_chunks // grp, body, (zero_state, zero_state))

    o_ref[...] = _head_rms(o_ref[...], g_ref[...])


def _hgrn(p, lower_bounds, norm_g, layer, batch, lb_rows, ctx_len, heads, off_q, off_ff, off_fb, off_v):
    t = p.shape[0]
    depth = lower_bounds.shape[0]

    def col(off):
        return pl.BlockSpec((lb_rows, HEAD_DIM), lambda b, h: (b, off // HEAD_DIM + h))

    kern = functools.partial(_hgrn_kernel, layer=layer, ctx_len=ctx_len)
    return pl.pallas_call(
        kern,
        grid=(batch, heads),
        in_specs=[col(off_q), col(off_ff), col(off_fb), col(off_v),
                  pl.BlockSpec((depth, 2, HEAD_DIM), lambda b, h: (0, 0, h)),
                  pl.BlockSpec((None, 1, HEAD_DIM), lambda b, h: (layer, 0, 0))],
        out_specs=pl.BlockSpec((lb_rows, HEAD_DIM), lambda b, h: (b, h)),
        out_shape=jax.ShapeDtypeStruct((t, heads * HEAD_DIM), f32),
        compiler_params=_cparams("parallel", "parallel"),
    )(p, p, p, p, lower_bounds, norm_g.reshape(depth, 1, HEAD_DIM))


def _short_conv_tile(u, cw, width, seq):
    r = u.shape[0]
    pos = lax.broadcasted_iota(jnp.int32, (r, 1), 0) % seq
    half = width // 2
    acc = u * cw[half:half + 1]
    for j in range(width):
        if j == half:
            continue
        uj = pltpu.roll(u, (half - j) % r, 0)
        valid = pos >= (half - j) if j < half else pos < seq - (j - half)
        acc = acc + jnp.where(valid, uj, 0.0) * cw[j:j + 1]
    return acc


def _dn_kernel(alog_ref, dt_ref, q_ref, k_ref, v_ref, ab_ref, cwq_ref, cwk_ref, cwv_ref, g_ref, o_ref,
               qn, kn, vn, mt_s, nt_s, qe_s, oe_s, gl_s, *, layer, ctx_len, heads):
    c = CHUNK
    h = pl.program_id(1)
    lb_rows = q_ref.shape[0]
    n_chunks = lb_rows // c
    n_ctx = ctx_len // c
    width = cwq_ref.shape[0]

    def prep(r0, rows, seq):
        for src, cw, dst, mode in ((q_ref, cwq_ref, qn, "q"), (k_ref, cwk_ref, kn, "k"), (v_ref, cwv_ref, vn, "v")):
            y = _silu(_short_conv_tile(src[pl.ds(r0, rows), :], cw[...], width, seq))
            if mode != "v":
                y = y * lax.rsqrt(jnp.sum(y * y, axis=-1, keepdims=True) + EPS)
            if mode == "q":
                y = y * (HEAD_DIM ** -0.5)
            dst[pl.ds(r0, rows), :] = y

    prep(0, ctx_len, ctx_len)
    rows_x = _largest_tile(lb_rows - ctx_len, c, 4 * c)

    def prep_body(i, carry):
        prep(pl.multiple_of(ctx_len + i * rows_x, c), rows_x, c)
        return carry

    lax.fori_loop(0, (lb_rows - ctx_len) // rows_x, prep_body, 0)

    sel_r = lax.broadcasted_iota(jnp.int32, (3 * LANES, 4 * LANES), 0) % LANES
    sel_c = lax.broadcasted_iota(jnp.int32, (3 * LANES, 4 * LANES), 1) // LANES
    ab_sel = jnp.where(sel_r == sel_c * heads + h, 1.0, 0.0).astype(bf16)

    row = lax.broadcasted_iota(jnp.int32, (c, LANES), 0)
    lane = lax.broadcasted_iota(jnp.int32, (c, LANES), 1)
    left_lanes = lane < c
    eye_right = jnp.where(lane == row + c, 1.0, 0.0)
    row3 = lax.broadcasted_iota(jnp.int32, (c, 3 * c), 0)
    lane3 = lax.broadcasted_iota(jnp.int32, (c, 3 * c), 1) % c
    ones3 = jnp.ones((c, 3 * c), bf16)
    zeros_k = jnp.zeros((c, HEAD_DIM), f32)
    dirs = []
    for d in (0, 1):
        if d == 0:
            m_incl, m_strict, m_incl_t, sel3 = lane <= row, lane < row, (row <= lane) & left_lanes, lane3 <= row3
        else:
            m_incl, m_strict, m_incl_t, sel3 = (lane >= row) & left_lanes, (lane > row) & left_lanes, lane <= row, lane3 >= row3
        neg_a = -jnp.exp(jnp.full((1, 1), alog_ref[layer, d, h], f32))
        dirs.append((m_incl, m_strict, m_incl_t, jnp.where(sel3, 1.0, 0.0).astype(bf16), neg_a, dt_ref[layer, d, h]))

    grp = _group_size(n_chunks, n_ctx)

    def pre_body(it, carry):
        chains = []
        for j in range(grp):
            ci = it * grp + j
            r0 = pl.multiple_of(ci * c, c)
            q = qn[pl.ds(r0, c), :]
            k = kn[pl.ds(r0, c), :]
            kq = _dot_nt(jnp.concatenate([k, q], axis=0), jnp.concatenate([k, zeros_k], axis=0))
            ab = jnp.dot(jnp.concatenate(_split3(ab_ref[pl.ds(r0, c), :]), axis=1), ab_sel,
                         preferred_element_type=f32)
            for d in (0, 1):
                chains.append(dict(d=d, ci=ci, r0=r0, q=q, k=k, kq=kq, a_in=ab[:, d * LANES:(d + 1) * LANES],
                                   b_in=ab[:, (2 + d) * LANES:(3 + d) * LANES]))
        for x in chains:
            _, _, m_incl_t, sel3, neg_a, dt = dirs[x["d"]]
            za = x["a_in"] + dt
            g = neg_a * (jnp.maximum(za, 0.0) + jnp.log(1.0 + jnp.exp(-jnp.abs(za))))
            x["beta"] = _sigmoid(x["b_in"])
            x["gc"] = _dot_sel(sel3, g)
            x["gc_row"] = _dot_sel(ones3, jnp.where(m_incl_t, g, 0.0))
        for x in chains:
            m_incl, m_strict = dirs[x["d"]][:2]
            x["gamma"] = jnp.where(m_incl, jnp.exp(jnp.minimum(x["gc"] - x["gc_row"], 0.0)), 0.0)
            x["x"] = eye_right - jnp.where(m_strict, x["beta"] * x["kq"][:c] * x["gamma"], 0.0)
        span = 1
        while span < c:
            for x in chains:
                xh, xl = _split2(x["x"])
                x["r"] = (jnp.dot(xh[:, :c], jnp.concatenate([xh, xl], axis=1), preferred_element_type=f32),
                          jnp.dot(xl[:, :c], xh, preferred_element_type=f32))
            for x in chains:
                r, r_lo = x["r"]
                x["x"] = r[:, :LANES] + r[:, LANES:] + r_lo + jnp.where(left_lanes, 0.0, x["x"])
            span *= 2
        for x in chains:
            r0 = x["r0"]
            x["egc"] = jnp.exp(x["gc"])
            kb = x["k"] * x["beta"]
            v = vn[pl.ds(r0, c), :]
            x["sol"] = _dot_hi(x["x"][:, c:], jnp.concatenate([v * x["beta"], kb * x["egc"]], axis=1))
        for x in chains:
            gc = x["gc"]
            x["gc_end"] = gc[0:1] if x["d"] else gc[c - 1:c]
            x["kw"] = _dot_tn(x["k"] * jnp.exp(x["gc_end"] - gc), x["sol"])
            x["qw"] = _dot((x["kq"][c:] * x["gamma"])[:, :c], x["sol"])
        for x in chains:
            d, r0, ci = x["d"], x["r0"], x["ci"]
            nt_s[d, ci] = x["kw"][:, :HEAD_DIM]
            mt_s[d, ci] = -x["kw"][:, HEAD_DIM:]
            oe_s[d, pl.ds(r0, c), :] = x["qw"][:, :HEAD_DIM]
            qe_s[d, pl.ds(r0, c), :] = x["q"] * x["egc"] - x["qw"][:, HEAD_DIM:]
            gl_s[d, ci] = jnp.broadcast_to(x["gc_end"], (SUB, LANES))
        return carry

    lax.fori_loop(0, n_chunks // grp, pre_body, 0)

    o_ref[...] = jnp.zeros_like(o_ref)

    def scan_body(i, sts):
        cis = [_chunk_index(i, n_ctx, n_chunks, d == 1) for d in (0, 1)]
        r0s = [pl.multiple_of(ci * c, c) for ci in cis]
        lin = [_dot(jnp.concatenate([mt_s[d, cis[d]], qe_s[d, pl.ds(r0s[d], c), :]], axis=0), sts[d]) for d in (0, 1)]
        for d in (0, 1):
            o_ref[pl.ds(r0s[d], c), :] += lin[d][HEAD_DIM:] + oe_s[d, pl.ds(r0s[d], c), :]
        return tuple(jnp.exp(gl_s[d, cis[d]][0:1]) * sts[d] + lin[d][:HEAD_DIM] + nt_s[d, cis[d]] for d in (0, 1))

    zero_state = jnp.zeros((HEAD_DIM, HEAD_DIM), f32)
    lax.fori_loop(0, n_chunks, scan_body, (zero_state, zero_state))

    o_ref[...] = _head_rms(o_ref[...], g_ref[...])


def _dn(p, conv_w, a_log, dt_bias, norm_g, layer, batch, lb_rows, ctx_len, heads, off_q, off_k, off_v, off_ab):
    t = p.shape[0]
    depth, width, _ = conv_w.shape
    kb_cols = heads * HEAD_DIM
    n_chunks = lb_rows // CHUNK

    def col(off):
        return pl.BlockSpec((lb_rows, HEAD_DIM), lambda b, h: (b, off // HEAD_DIM + h))

    def cw(off):
        return pl.BlockSpec((None, width, HEAD_DIM), lambda b, h: (layer, 0, off // HEAD_DIM + h))

    smem = pl.BlockSpec(memory_space=pltpu.SMEM)
    kern = functools.partial(_dn_kernel, layer=layer, ctx_len=ctx_len, heads=heads)
    row_buf = pltpu.VMEM((lb_rows, HEAD_DIM), f32)
    dir_buf = pltpu.VMEM((2, lb_rows, HEAD_DIM), f32)
    state_buf = pltpu.VMEM((2, n_chunks, HEAD_DIM, HEAD_DIM), f32)
    return pl.pallas_call(
        kern,
        grid=(batch, heads),
        in_specs=[smem, smem, col(off_q), col(off_k), col(off_v),
                  pl.BlockSpec((lb_rows, LANES), lambda b, h: (b, off_ab // LANES)),
                  cw(0), cw(kb_cols), cw(2 * kb_cols),
                  pl.BlockSpec((None, 1, HEAD_DIM), lambda b, h: (layer, 0, 0))],
        out_specs=pl.BlockSpec((lb_rows, HEAD_DIM), lambda b, h: (b, h)),
        out_shape=jax.ShapeDtypeStruct((t, heads * HEAD_DIM), f32),
        scratch_shapes=[row_buf] * 3 + [state_buf] * 2 + [dir_buf] * 2 + [pltpu.VMEM((2, n_chunks, SUB, LANES), f32)],
        compiler_params=_cparams("parallel", "parallel"),
    )(a_log, dt_bias, p, p, p, p, conv_w, conv_w, conv_w, norm_g.reshape(depth, 1, HEAD_DIM))


def _merge_kernel(s_ref, mod_ref, oa_ref, ob_ref, ag_ref, bz_ref, ga_ref, gb_ref, wa_ref, wb_ref, wo_ref, o_ref,
                  *, tiles_per_batch, ctx_len):
    is_ctx = _is_ctx_rows(pl.program_id(0), tiles_per_batch, s_ref.shape[0], ctx_len)
    ya = _bf(oa_ref[...] * _silu(ag_ref[...]))
    yb = _bf(ob_ref[...] * _silu(bz_ref[...]))
    y = (_sigmoid(ga_ref[...]) * jnp.dot(ya, wa_ref[...], preferred_element_type=f32)
         + _sigmoid(gb_ref[...]) * jnp.dot(yb, wb_ref[...], preferred_element_type=f32))
    mod = mod_ref[...]
    gate = jnp.where(is_ctx, mod[5:6], mod[2:3])
    o_ref[...] = s_ref[...] + gate * _dot(y, wo_ref[...])


def _merge(s, mods, oa, ob, p, wa, wb, wo, layer, lb, ctx_len, off_ag, off_bz):
    t, d = s.shape
    wa_cols = oa.shape[1]
    wb_cols = ob.shape[1]
    tm = _largest_tile(lb, SUB, 256)
    tpb = lb // tm
    kern = functools.partial(_merge_kernel, tiles_per_batch=tpb, ctx_len=ctx_len)

    def whole(a):
        return pl.BlockSpec((None,) + a.shape[1:], lambda i: (layer, 0, 0))

    return pl.pallas_call(
        kern,
        grid=(t // tm,),
        in_specs=[pl.BlockSpec((tm, d), lambda i: (i, 0)),
                  pl.BlockSpec((None, None, None, 8, d), lambda i: (layer, 1, i // tpb, 0, 0)),
                  pl.BlockSpec((tm, wa_cols), lambda i: (i, 0)),
                  pl.BlockSpec((tm, wb_cols), lambda i: (i, 0)),
                  pl.BlockSpec((tm, wa_cols), lambda i: (i, off_ag // wa_cols)),
                  pl.BlockSpec((tm, wb_cols), lambda i: (i, off_bz // wb_cols)),
                  pl.BlockSpec((tm, d), lambda i: (i, 0)),
                  pl.BlockSpec((tm, d), lambda i: (i, 1)),
                  whole(wa), whole(wb), whole(wo)],
        out_specs=pl.BlockSpec((tm, d), lambda i: (i, 0)),
        out_shape=jax.ShapeDtypeStruct((t, d), f32),
        compiler_params=_cparams("parallel"),
    )(s, mods, oa, ob, p, p, p, p, wa, wb, wo)


def _final_kernel(s_ref, g_ref, o_ref):
    s = s_ref[...]
    o_ref[...] = s * lax.rsqrt(jnp.mean(s * s, axis=-1, keepdims=True) + EPS) * g_ref[...]


def _final_norm(s3, g, ctx_len, seq):
    batch, _, d = s3.shape
    tm = _largest_tile(ctx_len, SUB, 256)
    return pl.pallas_call(
        _final_kernel,
        grid=(batch, seq // tm),
        in_specs=[pl.BlockSpec((None, tm, d), lambda b, j: (b, ctx_len // tm + j, 0)),
                  pl.BlockSpec((1, d), lambda b, j: (0, 0))],
        out_specs=pl.BlockSpec((None, tm, d), lambda b, j: (b, j, 0)),
        out_shape=jax.ShapeDtypeStruct((batch, seq, d), f32),
        compiler_params=_cparams("parallel", "parallel"),
    )(s3, g.reshape(1, d))


def kernel(x, c, ctx, c_ctx, w_ada, b_ada, norm_g, final_norm_g, ffn_w_gate, ffn_w_up, ffn_w_down, w_in,
           hgrn_lower_bounds, hgrn_norm_g, dn_conv_w, dn_a_log, dn_dt_bias, dn_norm_g, w_branch_a, w_branch_b, w_out):
    batch, seq, d = x.shape
    ctx_len = ctx.shape[1]
    depth = w_ada.shape[0]
    lb = ctx_len + seq
    ka = hgrn_lower_bounds.shape[-1]
    wa = w_branch_a.shape[1]
    wb = w_branch_b.shape[1]
    kb = (dn_conv_w.shape[-1] - wb) // 2
    heads = dn_a_log.shape[-1]
    assert ka == wa == kb == wb == heads * HEAD_DIM and seq % CHUNK == 0 and ctx_len % CHUNK == 0
    assert d % wa == 0 and 4 * heads <= LANES and batch + 1 <= 8

    cc = jnp.concatenate([c, c_ctx[None], jnp.zeros((8 - batch - 1, d), f32)], axis=0)
    m = _adaln(cc, w_ada, b_ada).reshape(depth, 8, 3, 3, d)
    m_own = jnp.transpose(m[:, :batch], (0, 2, 1, 3, 4))
    m_ctx = jnp.broadcast_to(m[:, batch][:, :, None], (depth, 3, batch, 3, d))
    mods = jnp.concatenate([m_own, m_ctx, jnp.zeros((depth, 3, batch, 2, d), f32)], axis=3)
    norm_g4 = norm_g.reshape(depth, 3, 1, d)

    o_ba = 3 * ka + 2 * wa + 2 * kb + wb + wb
    o_ga = o_ba + 4 * heads
    w_cols = jnp.concatenate([w_in[:, :, o_ga:], w_in[:, :, :o_ba], w_in[:, :, o_ba:o_ga],
                              jnp.zeros((depth, d, LANES - 4 * heads), f32)], axis=2).astype(bf16)
    base = 2 * d
    off_q, off_ff, off_fb, off_v, off_ag = base, base + ka, base + 2 * ka, base + 3 * ka, base + 3 * ka + wa
    off_bq = base + 3 * ka + 2 * wa
    off_bk, off_bv, off_bz = off_bq + kb, off_bq + 2 * kb, off_bq + 2 * kb + wb
    off_ab = base + o_ba

    wg, wu, wd = ffn_w_gate.astype(bf16), ffn_w_up.astype(bf16), ffn_w_down.astype(bf16)
    wbr_a, wbr_b, wo = w_branch_a.astype(bf16), w_branch_b.astype(bf16), w_out.astype(bf16)

    s = jnp.concatenate([ctx, x], axis=1).reshape(batch * lb, d)
    for l in range(depth):
        s = _ffn(s, norm_g4, mods, wg, wu, wd, l, 0, 0, lb, ctx_len)
        p = _proj(s, norm_g4, mods, w_cols, l, lb, ctx_len)
        oa = _hgrn(p, hgrn_lower_bounds, hgrn_norm_g, l, batch, lb, ctx_len, heads, off_q, off_ff, off_fb, off_v)
        ob = _dn(p, dn_conv_w, dn_a_log, dn_dt_bias, dn_norm_g, l, batch, lb, ctx_len, heads, off_bq, off_bk, off_bv,
                 off_ab)
        s = _merge(s, mods, oa, ob, p, wbr_a, wbr_b, wo, l, lb, ctx_len, off_ag, off_bz)
        s = _ffn(s, norm_g4, mods, wg, wu, wd, l, 2, 1, lb, ctx_len)
    return _final_norm(s.reshape(batch, lb, d), final_norm_g, ctx_len, seq)
```

```python
import functools

import jax
import jax.numpy as jnp
from jax import lax
from jax.experimental import pallas as pl
from jax.experimental.pallas import tpu as pltpu

EPS = 1e-6
FFN_RES = 0.5
HEAD_DIM = 128
CHUNK = 64
SUB = 8
ROW_BLOCK = 64
LANES = 128
VMEM_LIMIT_BYTES = 56 * 1024 * 1024

f32 = jnp.float32
bf16 = jnp.bfloat16


def _cparams(*sem):
    return pltpu.CompilerParams(dimension_semantics=sem, vmem_limit_bytes=VMEM_LIMIT_BYTES)


def _bf(a):
    return a.astype(bf16)


def _dot(a, b):
    return jnp.dot(_bf(a), _bf(b), preferred_element_type=f32)


def _dot_nt(a, b):
    return lax.dot_general(_bf(a), _bf(b), (((1,), (1,)), ((), ())), preferred_element_type=f32)


def _dot_tn(a, b):
    return lax.dot_general(_bf(a), _bf(b), (((0,), (0,)), ((), ())), preferred_element_type=f32)


def _split2(a):
    hi = _bf(a)
    lo = _bf(a - hi.astype(f32))
    return hi, lo


def _split3(a):
    hi = _bf(a)
    r = a - hi.astype(f32)
    mid = _bf(r)
    lo = _bf(r - mid.astype(f32))
    return hi, mid, lo


def _dot_sel(m01x3, a):
    return jnp.dot(m01x3, jnp.concatenate(_split3(a), axis=0), preferred_element_type=f32)


def _sigmoid(a):
    return 1.0 / (1.0 + jnp.exp(-a))


def _silu(a):
    return a * _sigmoid(a)


def _largest_tile(total, unit, cap):
    n = total // unit
    best = 1
    for k in range(1, n + 1):
        if n % k == 0 and k * unit <= cap:
            best = k
    return best * unit


def _adaln_kernel(c_ref, w_ref, b_ref, o_ref):
    a = _silu(c_ref[...])
    o_ref[...] = _dot(a, w_ref[...]) + b_ref[...]


def _adaln(cc, w_ada, b_ada):
    depth, d, n = w_ada.shape
    rows = cc.shape[0]
    tn = _largest_tile(n, LANES, 1024)
    return pl.pallas_call(
        _adaln_kernel,
        grid=(depth, n // tn),
        in_specs=[pl.BlockSpec((rows, d), lambda l, j: (0, 0)),
                  pl.BlockSpec((None, d, tn), lambda l, j: (l, 0, j)),
                  pl.BlockSpec((None, 1, tn), lambda l, j: (l, 0, j))],
        out_specs=pl.BlockSpec((None, rows, tn), lambda l, j: (l, 0, j)),
        out_shape=jax.ShapeDtypeStruct((depth, rows, n), f32),
        compiler_params=_cparams("parallel", "parallel"),
    )(cc, w_ada, b_ada.reshape(depth, 1, n))


def _is_ctx_rows(tile_idx, tiles_per_batch, tm, ctx_len):
    row0 = (tile_idx % tiles_per_batch) * tm
    rows = row0 + lax.broadcasted_iota(jnp.int32, (tm, 1), 0)
    return rows < ctx_len


def _row_block_loops(tm, tile_idx, tiles_per_batch, ctx_len, body):
    n_blk = tm // ROW_BLOCK
    row0 = (tile_idx % tiles_per_batch) * tm
    n_ctx = jnp.clip((ctx_len - row0) // ROW_BLOCK, 0, n_blk)
    for lo, hi, is_ctx in ((0, n_ctx, True), (n_ctx, n_blk, False)):
        def step(j, carry, is_ctx=is_ctx):
            body(pl.multiple_of(j * ROW_BLOCK, ROW_BLOCK), is_ctx)
            return carry
        lax.fori_loop(lo, hi, step, 0)


def _norm_mod_store(s_ref, g_ref, mod_ref, h_ref, tile_idx, tiles_per_batch, ctx_len):
    tm, d = s_ref.shape
    g = g_ref[...]
    mod = mod_ref[...]
    sub = ROW_BLOCK // 4
    gains = {True: jnp.broadcast_to(g * (1.0 + mod[4:5]), (sub, d)), False: jnp.broadcast_to(g * (1.0 + mod[1:2]), (sub, d))}
    shifts = {True: jnp.broadcast_to(mod[3:4], (sub, d)), False: jnp.broadcast_to(mod[0:1], (sub, d))}

    def body(r, is_ctx):
        for k in range(4):
            rk = r + k * sub
            x = s_ref[pl.ds(rk, sub), :]
            rs = lax.rsqrt(jnp.mean(x * x, axis=-1, keepdims=True) + EPS)
            h_ref[pl.ds(rk, sub), :] = _bf(x * rs * gains[is_ctx] + shifts[is_ctx])

    _row_block_loops(tm, tile_idx, tiles_per_batch, ctx_len, body)


def _residual_store(s_ref, mod_ref, o_ref, weight, tile_idx, tiles_per_batch, ctx_len):
    tm, d = s_ref.shape
    mod = mod_ref[...]
    gates = {True: jnp.broadcast_to(weight * mod[5:6], (ROW_BLOCK, d)), False: jnp.broadcast_to(weight * mod[2:3], (ROW_BLOCK, d))}

    def body(r, is_ctx):
        rows = pl.ds(r, ROW_BLOCK)
        o_ref[rows, :] = s_ref[rows, :] + gates[is_ctx] * o_ref[rows, :]

    _row_block_loops(tm, tile_idx, tiles_per_batch, ctx_len, body)


def _ffn_kernel(s_ref, g_ref, mod_ref, wg_ref, wu_ref, wd_ref, o_ref, h_ref, *, tiles_per_batch, ctx_len):
    i = pl.program_id(0)
    f = pl.program_id(1)

    @pl.when(f == 0)
    def _():
        _norm_mod_store(s_ref, g_ref, mod_ref, h_ref, i, tiles_per_batch, ctx_len)
        o_ref[...] = jnp.zeros_like(o_ref)

    h = h_ref[...]
    a = jnp.dot(h, wg_ref[...], preferred_element_type=f32)
    u = jnp.dot(h, wu_ref[...], preferred_element_type=f32)
    o_ref[...] += _dot(_silu(a) * u, wd_ref[...])

    @pl.when(f == pl.num_programs(1) - 1)
    def _():
        _residual_store(s_ref, mod_ref, o_ref, FFN_RES, i, tiles_per_batch, ctx_len)


def _ffn(s, g, mods, wg, wu, wd, layer, sub, which, lb, ctx_len):
    t, d = s.shape
    dff = wg.shape[-1]
    tm = _largest_tile(lb, SUB, 576)
    tf = _largest_tile(dff, LANES, 512)
    tpb = lb // tm
    kern = functools.partial(_ffn_kernel, tiles_per_batch=tpb, ctx_len=ctx_len)
    return pl.pallas_call(
        kern,
        grid=(t // tm, dff // tf),
        in_specs=[pl.BlockSpec((tm, d), lambda i, f: (i, 0)),
                  pl.BlockSpec((None, None, 1, d), lambda i, f: (layer, sub, 0, 0)),
                  pl.BlockSpec((None, None, None, 8, d), lambda i, f: (layer, sub, i // tpb, 0, 0)),
                  pl.BlockSpec((None, None, d, tf), lambda i, f: (layer, which, 0, f)),
                  pl.BlockSpec((None, None, d, tf), lambda i, f: (layer, which, 0, f)),
                  pl.BlockSpec((None, None, tf, d), lambda i, f: (layer, which, f, 0))],
        out_specs=pl.BlockSpec((tm, d), lambda i, f: (i, 0)),
        out_shape=jax.ShapeDtypeStruct((t, d), f32),
        scratch_shapes=[pltpu.VMEM((tm, d), bf16)],
        compiler_params=_cparams("parallel", "arbitrary"),
    )(s, g, mods, wg, wu, wd)


def _proj_kernel(s_ref, g_ref, mod_ref, w_ref, o_ref, h_ref, *, tiles_per_batch, ctx_len):
    @pl.when(pl.program_id(1) == 0)
    def _():
        _norm_mod_store(s_ref, g_ref, mod_ref, h_ref, pl.program_id(0), tiles_per_batch, ctx_len)

    o_ref[...] = jnp.dot(h_ref[...], w_ref[...], preferred_element_type=f32)


def _proj(s, g, mods, w, layer, lb, ctx_len):
    t, d = s.shape
    n = w.shape[-1]
    tm = _largest_tile(lb, SUB, 1152)
    tn = _largest_tile(n, LANES, 1024)
    tpb = lb // tm
    kern = functools.partial(_proj_kernel, tiles_per_batch=tpb, ctx_len=ctx_len)
    return pl.pallas_call(
        kern,
        grid=(t // tm, n // tn),
        in_specs=[pl.BlockSpec((tm, d), lambda i, j: (i, 0)),
                  pl.BlockSpec((None, None, 1, d), lambda i, j: (layer, 1, 0, 0)),
                  pl.BlockSpec((None, None, None, 8, d), lambda i, j: (layer, 1, i // tpb, 0, 0)),
                  pl.BlockSpec((None, d, tn), lambda i, j: (layer, 0, j))],
        out_specs=pl.BlockSpec((tm, tn), lambda i, j: (i, j)),
        out_shape=jax.ShapeDtypeStruct((t, n), f32),
        scratch_shapes=[pltpu.VMEM((tm, d), bf16)],
        compiler_params=_cparams("parallel", "arbitrary"),
    )(s, g, mods, w)


def _chunk_index(i, n_ctx, n_chunks, reverse):
    if not reverse:
        return i
    return jnp.where(i < n_ctx, n_ctx - 1 - i, n_chunks + n_ctx - 1 - i)


def _group_size(n_chunks, n_ctx):
    for g in (4, 2):
        if n_chunks % g == 0 and n_ctx % g == 0:
            return g
    return 1


def _causal_masks(c, reverse):
    t = lax.broadcasted_iota(jnp.int32, (c, c), 0)
    s = lax.broadcasted_iota(jnp.int32, (c, c), 1)
    if reverse:
        return s >= t, s > t, t, s
    return s <= t, s < t, t, s


def _head_rms(o, g):
    return o * lax.rsqrt(jnp.mean(o * o, axis=-1, keepdims=True) + EPS) * g


def _hgrn_group(items, sts, consts):
    c = items[0]["q"].shape[0]
    sts = list(sts)
    for x in items:
        x["b"] = _dot_sel(consts[x["d"]][0], x["lc"])
    for x in items:
        d, b = x["d"], x["b"]
        b_end = b[0:1] if d else b[c - 1:c]
        x["o"] = _dot_nt(x["q"] * jnp.exp(b), sts[d])
        sts[d] = sts[d] * jnp.exp(b_end) + _dot_tn(x["v"], x["k"] * jnp.exp(b_end - b))

    for x in items:
        x["attn"] = jnp.zeros((c, c), f32)
    m, level = c // 2, 0
    while m >= SUB:
        n = c // m
        zeros_blk = jnp.zeros((m, LANES), f32)
        for x in items:
            d, b, q, k = x["d"], x["b"], x["q"], x["k"]
            qs, ks = [], []
            for i in range(n):
                lo, hi = i * m, (i + 1) * m
                if (i % 2 == 1) != bool(d):
                    edge = b[hi - 1:hi] if d else b[lo:lo + 1]
                    qs.append(q[lo:hi] * jnp.exp(jnp.minimum(b[lo:hi] - edge, 0.0)))
                    ks.append(zeros_blk)
                else:
                    edge = b[lo - 1:lo] if d else b[hi:hi + 1]
                    ks.append(k[lo:hi] * jnp.exp(jnp.minimum(edge - b[lo:hi], 0.0)))
                    qs.append(zeros_blk)
            x["lv"] = _dot_nt(jnp.concatenate(qs, axis=0), jnp.concatenate(ks, axis=0))
        for x in items:
            x["attn"] = x["attn"] + jnp.where(consts[x["d"]][1][level], x["lv"], 0.0)
        m, level = m // 2, level + 1
    for x in items:
        x["ov"] = _dot(x["attn"], x["v"])

    n = c // SUB
    for x in items:
        sub_mask, ones_red = consts[x["d"]][2:]
        b3 = x["b"].reshape(n, SUB, LANES)
        q3 = x["q"].reshape(n, SUB, LANES)
        k3 = x["k"].reshape(n, SUB, LANES)
        prods = [(q3 * k3).reshape(c, LANES)]
        for j in range(1, SUB):
            shift = SUB - j if x["d"] else j
            e = jnp.exp(b3 - pltpu.roll(b3, shift, 1))
            p = jnp.where(sub_mask[j], q3 * e * pltpu.roll(k3, shift, 1), 0.0)
            prods.append(p.reshape(c, LANES))
        x["sums"] = jnp.dot(_bf(jnp.concatenate(prods, axis=0)), ones_red, preferred_element_type=f32)
    for x in items:
        v3 = x["v"].reshape(n, SUB, LANES)
        od = x["sums"][:c].reshape(n, SUB, LANES) * v3
        for j in range(1, SUB):
            shift = SUB - j if x["d"] else j
            od = od + x["sums"][j * c:(j + 1) * c].reshape(n, SUB, LANES) * pltpu.roll(v3, shift, 1)
        x["o"] = x["o"] + x["ov"] + od.reshape(c, LANES)
    return sts


def _hgrn_consts(c, reverse):
    m_incl, _, t, s = _causal_masks(c, reverse)
    level_masks = []
    m = c // 2
    while m >= SUB:
        tb, sb = t // m, s // m
        if reverse:
            level_masks.append((tb % 2 == 0) & (sb == tb + 1))
        else:
            level_masks.append((tb % 2 == 1) & (sb == tb - 1))
        m //= 2
    r = lax.broadcasted_iota(jnp.int32, (1, SUB, LANES), 1)
    sub_mask = [(r <= SUB - 1 - j) if reverse else (r >= j) for j in range(SUB)]
    row3 = lax.broadcasted_iota(jnp.int32, (c, 3 * c), 0)
    lane3 = lax.broadcasted_iota(jnp.int32, (c, 3 * c), 1) % c
    sel3 = jnp.where((lane3 >= row3) if reverse else (lane3 <= row3), 1.0, 0.0).astype(bf16)
    return (sel3, level_masks, sub_mask, jnp.ones((LANES, LANES), bf16))


def _hgrn_kernel(q_ref, ff_ref, fb_ref, v_ref, lb_ref, g_ref, o_ref, *, layer, ctx_len):
    c = CHUNK
    n_chunks = q_ref.shape[0] // c
    n_ctx = ctx_len // c

    raw = lb_ref[...]
    e = jnp.exp(raw - jnp.max(raw, axis=0, keepdims=True))
    sm = e / jnp.sum(e, axis=0, keepdims=True)
    lbs = jnp.sum(sm[:layer + 1], axis=0) - sm[0]

    consts = [_hgrn_consts(c, False), _hgrn_consts(c, True)]
    grp = _group_size(n_chunks, n_ctx)
    o_ref[...] = jnp.zeros_like(o_ref)

    def body(it, sts):
        items = []
        for j in range(grp):
            for d in (0, 1):
                r0 = pl.multiple_of(_chunk_index(it * grp + j, n_ctx, n_chunks, d == 1) * c, c)
                xf = (fb_ref if d else ff_ref)[pl.ds(r0, c), :]
                lb = lbs[d:d + 1]
                items.append(dict(d=d, r0=r0, q=_silu(q_ref[pl.ds(r0, c), :]), v=v_ref[pl.ds(r0, c), :],
                                  lc=jnp.log(lb + (1.0 - lb) * _sigmoid(xf)), k=(1.0 - lb) * _sigmoid(-xf)))
        sts = _hgrn_group(items, sts, consts)
        for x in items:
            o_ref[pl.ds(x["r0"], c), :] += x["o"]
        return tuple(sts)

    zero_state = jnp.zeros((HEAD_DIM, HEAD_DIM), f32)
    lax.fori_loop(0, n_chunks // grp, body, (zero_state, zero_state))

    o_ref[...] = _head_rms(o_ref[...], g_ref[...])


def _hgrn(p, lower_bounds, norm_g, layer, batch, lb_rows, ctx_len, heads, off_q, off_ff, off_fb, off_v):
    t = p.shape[0]
    depth = lower_bounds.shape[0]

    def col(off):
        return pl.BlockSpec((lb_rows, HEAD_DIM), lambda b, h: (b, off // HEAD_DIM + h))

    kern = functools.partial(_hgrn_kernel, layer=layer, ctx_len=ctx_len)
    return pl.pallas_call(
        kern,
        grid=(batch, heads),
        in_specs=[col(off_q), col(off_ff), col(off_fb), col(off_v),
                  pl.BlockSpec((depth, 2, HEAD_DIM), lambda b, h: (0, 0, h)),
                  pl.BlockSpec((None, 1, HEAD_DIM), lambda b, h: (layer, 0, 0))],
        out_specs=pl.BlockSpec((lb_rows, HEAD_DIM), lambda b, h: (b, h)),
        out_shape=jax.ShapeDtypeStruct((t, heads * HEAD_DIM), f32),
        compiler_params=_cparams("parallel", "parallel"),
    )(p, p, p, p, lower_bounds, norm_g.reshape(depth, 1, HEAD_DIM))


def _short_conv_tile(u, cw, width, seq):
    r = u.shape[0]
    pos = lax.broadcasted_iota(jnp.int32, (r, 1), 0) % seq
    half = width // 2
    acc = u * cw[half:half + 1]
    for j in range(width):
        if j == half:
            continue
        uj = pltpu.roll(u, (half - j) % r, 0)
        valid = pos >= (half - j) if j < half else pos < seq - (j - half)
        acc = acc + jnp.where(valid, uj, 0.0) * cw[j:j + 1]
    return acc


def _dn_kernel(alog_ref, dt_ref, q_ref, k_ref, v_ref, ab_ref, cwq_ref, cwk_ref, cwv_ref, g_ref, o_ref,
               qn, kn, vn, mt_s, nt_s, qe_s, oe_s, gl_s, *, layer, ctx_len, heads):
    c = CHUNK
    h = pl.program_id(1)
    lb_rows = q_ref.shape[0]
    n_chunks = lb_rows // c
    n_ctx = ctx_len // c
    width = cwq_ref.shape[0]

    def prep(r0, rows, seq):
        for src, cw, dst, mode in ((q_ref, cwq_ref, qn, "q"), (k_ref, cwk_ref, kn, "k"), (v_ref, cwv_ref, vn, "v")):
            y = _silu(_short_conv_tile(src[pl.ds(r0, rows), :], cw[...], width, seq))
            if mode != "v":
                y = y * lax.rsqrt(jnp.sum(y * y, axis=-1, keepdims=True) + EPS)
            if mode == "q":
                y = y * (HEAD_DIM ** -0.5)
            dst[pl.ds(r0, rows), :] = y

    prep(0, ctx_len, ctx_len)
    rows_x = _largest_tile(lb_rows - ctx_len, c, 4 * c)

    def prep_body(i, carry):
        prep(pl.multiple_of(ctx_len + i * rows_x, c), rows_x, c)
        return carry

    lax.fori_loop(0, (lb_rows - ctx_len) // rows_x, prep_body, 0)

    sel_r = lax.broadcasted_iota(jnp.int32, (3 * LANES, 4 * LANES), 0) % LANES
    sel_c = lax.broadcasted_iota(jnp.int32, (3 * LANES, 4 * LANES), 1) // LANES
    ab_sel = jnp.where(sel_r == sel_c * heads + h, 1.0, 0.0).astype(bf16)

    row = lax.broadcasted_iota(jnp.int32, (c, LANES), 0)
    lane = lax.broadcasted_iota(jnp.int32, (c, LANES), 1)
    left_lanes = lane < c
    eye_right = jnp.where(lane == row + c, 1.0, 0.0)
    row3 = lax.broadcasted_iota(jnp.int32, (c, 3 * c), 0)
    lane3 = lax.broadcasted_iota(jnp.int32, (c, 3 * c), 1) % c
    ones3 = jnp.ones((c, 3 * c), bf16)
    zeros_k = jnp.zeros((c, HEAD_DIM), f32)
    dirs = []
    for d in (0, 1):
        if d == 0:
            m_incl, m_strict, m_incl_t, sel3 = lane <= row, lane < row, (row <= lane) & left_lanes, lane3 <= row3
        else:
            m_incl, m_strict, m_incl_t, sel3 = (lane >= row) & left_lanes, (lane > row) & left_lanes, lane <= row, lane3 >= row3
        neg_a = -jnp.exp(jnp.full((1, 1), alog_ref[layer, d, h], f32))
        dirs.append((m_incl, m_strict, m_incl_t, jnp.where(sel3, 1.0, 0.0).astype(bf16), neg_a, dt_ref[layer, d, h]))

    grp = _group_size(n_chunks, n_ctx)

    def pre_body(it, carry):
        chains = []
        for j in range(grp):
            ci = it * grp + j
            r0 = pl.multiple_of(ci * c, c)
            q = qn[pl.ds(r0, c), :]
            k = kn[pl.ds(r0, c), :]
            kq = _dot_nt(jnp.concatenate([k, q], axis=0), jnp.concatenate([k, zeros_k], axis=0))
            ab = jnp.dot(jnp.concatenate(_split3(ab_ref[pl.ds(r0, c), :]), axis=1), ab_sel,
                         preferred_element_type=f32)
            for d in (0, 1):
                chains.append(dict(d=d, ci=ci, r0=r0, q=q, k=k, kq=kq, a_in=ab[:, d * LANES:(d + 1) * LANES],
                                   b_in=ab[:, (2 + d) * LANES:(3 + d) * LANES]))
        for x in chains:
            _, _, m_incl_t, sel3, neg_a, dt = dirs[x["d"]]
            za = x["a_in"] + dt
            g = neg_a * (jnp.maximum(za, 0.0) + jnp.log(1.0 + jnp.exp(-jnp.abs(za))))
            x["beta"] = _sigmoid(x["b_in"])
            x["gc"] = _dot_sel(sel3, g)
            x["gc_row"] = _dot_sel(ones3, jnp.where(m_incl_t, g, 0.0))
        for x in chains:
            m_incl, m_strict = dirs[x["d"]][:2]
            x["gamma"] = jnp.where(m_incl, jnp.exp(jnp.minimum(x["gc"] - x["gc_row"], 0.0)), 0.0)
            x["x"] = eye_right - jnp.where(m_strict, x["beta"] * x["kq"][:c] * x["gamma"], 0.0)
        span = 1
        while span < c:
            for x in chains:
                xh, xl = _split2(x["x"])
                x["r"] = (jnp.dot(xh[:, :c], jnp.concatenate([xh, xl], axis=1), preferred_element_type=f32),
                          jnp.dot(xl[:, :c], xh, preferred_element_type=f32))
            for x in chains:
                r, r_lo = x["r"]
                x["x"] = r[:, :LANES] + r[:, LANES:] + r_lo + jnp.where(left_lanes, 0.0, x["x"])
            span *= 2
        for x in chains:
            r0 = x["r0"]
            x["egc"] = jnp.exp(x["gc"])
            kb = x["k"] * x["beta"]
            v = vn[pl.ds(r0, c), :]
            ah, al = _split2(x["x"][:, c:])
            bh, bl = _split2(jnp.concatenate([v * x["beta"], kb * x["egc"]], axis=1))
            r = jnp.dot(ah, jnp.concatenate([bh, bl], axis=1), preferred_element_type=f32)
            x["sol"] = r[:, :2 * HEAD_DIM] + r[:, 2 * HEAD_DIM:] + jnp.dot(al, bh, preferred_element_type=f32)
        for x in chains:
            gc = x["gc"]
            x["gc_end"] = gc[0:1] if x["d"] else gc[c - 1:c]
            x["kw"] = _dot_tn(x["k"] * jnp.exp(x["gc_end"] - gc), x["sol"])
            x["qw"] = _dot((x["kq"][c:] * x["gamma"])[:, :c], x["sol"])
        for x in chains:
            d, r0, ci = x["d"], x["r0"], x["ci"]
            nt_s[d, ci] = x["kw"][:, :HEAD_DIM]
            mt_s[d, ci] = -x["kw"][:, HEAD_DIM:]
            oe_s[d, pl.ds(r0, c), :] = x["qw"][:, :HEAD_DIM]
            qe_s[d, pl.ds(r0, c), :] = x["q"] * x["egc"] - x["qw"][:, HEAD_DIM:]
            gl_s[d, ci] = jnp.broadcast_to(x["gc_end"], (SUB, LANES))
        return carry

    lax.fori_loop(0, n_chunks // grp, pre_body, 0)

    o_ref[...] = jnp.zeros_like(o_ref)

    def scan_body(i, sts):
        cis = [_chunk_index(i, n_ctx, n_chunks, d == 1) for d in (0, 1)]
        r0s = [pl.multiple_of(ci * c, c) for ci in cis]
        lin = [_dot(jnp.concatenate([mt_s[d, cis[d]], qe_s[d, pl.ds(r0s[d], c), :]], axis=0), sts[d]) for d in (0, 1)]
        for d in (0, 1):
            o_ref[pl.ds(r0s[d], c), :] += lin[d][HEAD_DIM:] + oe_s[d, pl.ds(r0s[d], c), :]
        return tuple(jnp.exp(gl_s[d, cis[d]][0:1]) * sts[d] + lin[d][:HEAD_DIM] + nt_s[d, cis[d]] for d in (0, 1))

    zero_state = jnp.zeros((HEAD_DIM, HEAD_DIM), f32)
    lax.fori_loop(0, n_chunks, scan_body, (zero_state, zero_state))

    o_ref[...] = _head_rms(o_ref[...], g_ref[...])


def _dn(p, conv_w, a_log, dt_bias, norm_g, layer, batch, lb_rows, ctx_len, heads, off_q, off_k, off_v, off_ab):
    t = p.shape[0]
    depth, width, _ = conv_w.shape
    kb_cols = heads * HEAD_DIM
    n_chunks = lb_rows // CHUNK

    def col(off):
        return pl.BlockSpec((lb_rows, HEAD_DIM), lambda b, h: (b, off // HEAD_DIM + h))

    def cw(off):
        return pl.BlockSpec((None, width, HEAD_DIM), lambda b, h: (layer, 0, off // HEAD_DIM + h))

    smem = pl.BlockSpec(memory_space=pltpu.SMEM)
    kern = functools.partial(_dn_kernel, layer=layer, ctx_len=ctx_len, heads=heads)
    row_buf = pltpu.VMEM((lb_rows, HEAD_DIM), f32)
    dir_buf = pltpu.VMEM((2, lb_rows, HEAD_DIM), f32)
    state_buf = pltpu.VMEM((2, n_chunks, HEAD_DIM, HEAD_DIM), f32)
    return pl.pallas_call(
        kern,
        grid=(batch, heads),
        in_specs=[smem, smem, col(off_q), col(off_k), col(off_v),
                  pl.BlockSpec((lb_rows, LANES), lambda b, h: (b, off_ab // LANES)),
                  cw(0), cw(kb_cols), cw(2 * kb_cols),
                  pl.BlockSpec((None, 1, HEAD_DIM), lambda b, h: (layer, 0, 0))],
        out_specs=pl.BlockSpec((lb_rows, HEAD_DIM), lambda b, h: (b, h)),
        out_shape=jax.ShapeDtypeStruct((t, heads * HEAD_DIM), f32),
        scratch_shapes=[row_buf] * 3 + [state_buf] * 2 + [dir_buf] * 2 + [pltpu.VMEM((2, n_chunks, SUB, LANES), f32)],
        compiler_params=_cparams("parallel", "parallel"),
    )(a_log, dt_bias, p, p, p, p, conv_w, conv_w, conv_w, norm_g.reshape(depth, 1, HEAD_DIM))


def _merge_kernel(s_ref, mod_ref, oa_ref, ob_ref, ag_ref, bz_ref, ga_ref, gb_ref, wa_ref, wb_ref, wo_ref, o_ref,
                  *, tiles_per_batch, ctx_len):
    is_ctx = _is_ctx_rows(pl.program_id(0), tiles_per_batch, s_ref.shape[0], ctx_len)
    ya = _bf(oa_ref[...] * _silu(ag_ref[...]))
    yb = _bf(ob_ref[...] * _silu(bz_ref[...]))
    y = (_sigmoid(ga_ref[...]) * jnp.dot(ya, wa_ref[...], preferred_element_type=f32)
         + _sigmoid(gb_ref[...]) * jnp.dot(yb, wb_ref[...], preferred_element_type=f32))
    mod = mod_ref[...]
    gate = jnp.where(is_ctx, mod[5:6], mod[2:3])
    o_ref[...] = s_ref[...] + gate * _dot(y, wo_ref[...])


def _merge(s, mods, oa, ob, p, wa, wb, wo, layer, lb, ctx_len, off_ag, off_bz):
    t, d = s.shape
    wa_cols = oa.shape[1]
    wb_cols = ob.shape[1]
    tm = _largest_tile(lb, SUB, 256)
    tpb = lb // tm
    kern = functools.partial(_merge_kernel, tiles_per_batch=tpb, ctx_len=ctx_len)

    def whole(a):
        return pl.BlockSpec((None,) + a.shape[1:], lambda i: (layer, 0, 0))

    return pl.pallas_call(
        kern,
        grid=(t // tm,),
        in_specs=[pl.BlockSpec((tm, d), lambda i: (i, 0)),
                  pl.BlockSpec((None, None, None, 8, d), lambda i: (layer, 1, i // tpb, 0, 0)),
                  pl.BlockSpec((tm, wa_cols), lambda i: (i, 0)),
                  pl.BlockSpec((tm, wb_cols), lambda i: (i, 0)),
                  pl.BlockSpec((tm, wa_cols), lambda i: (i, off_ag // wa_cols)),
                  pl.BlockSpec((tm, wb_cols), lambda i: (i, off_bz // wb_cols)),
                  pl.BlockSpec((tm, d), lambda i: (i, 0)),
                  pl.BlockSpec((tm, d), lambda i: (i, 1)),
                  whole(wa), whole(wb), whole(wo)],
        out_specs=pl.BlockSpec((tm, d), lambda i: (i, 0)),
        out_shape=jax.ShapeDtypeStruct((t, d), f32),
        compiler_params=_cparams("parallel"),
    )(s, mods, oa, ob, p, p, p, p, wa, wb, wo)


def _final_kernel(s_ref, g_ref, o_ref):
    s = s_ref[...]
    o_ref[...] = s * lax.rsqrt(jnp.mean(s * s, axis=-1, keepdims=True) + EPS) * g_ref[...]


def _final_norm(s3, g, ctx_len, seq):
    batch, _, d = s3.shape
    tm = _largest_tile(ctx_len, SUB, 256)
    return pl.pallas_call(
        _final_kernel,
        grid=(batch, seq // tm),
        in_specs=[pl.BlockSpec((None, tm, d), lambda b, j: (b, ctx_len // tm + j, 0)),
                  pl.BlockSpec((1, d), lambda b, j: (0, 0))],
        out_specs=pl.BlockSpec((None, tm, d), lambda b, j: (b, j, 0)),
        out_shape=jax.ShapeDtypeStruct((batch, seq, d), f32),
        compiler_params=_cparams("parallel", "parallel"),
    )(s3, g.reshape(1, d))


def kernel(x, c, ctx, c_ctx, w_ada, b_ada, norm_g, final_norm_g, ffn_w_gate, ffn_w_up, ffn_w_down, w_in,
           hgrn_lower_bounds, hgrn_norm_g, dn_conv_w, dn_a_log, dn_dt_bias, dn_norm_g, w_branch_a, w_branch_b, w_out):
    batch, seq, d = x.shape
    ctx_len = ctx.shape[1]
    depth = w_ada.shape[0]
    lb = ctx_len + seq
    ka = hgrn_lower_bounds.shape[-1]
    wa = w_branch_a.shape[1]
    wb = w_branch_b.shape[1]
    kb = (dn_conv_w.shape[-1] - wb) // 2
    heads = dn_a_log.shape[-1]
    assert ka == wa == kb == wb == heads * HEAD_DIM and seq % CHUNK == 0 and ctx_len % CHUNK == 0
    assert d % wa == 0 and 4 * heads <= LANES and batch + 1 <= 8 and ctx_len % ROW_BLOCK == 0 and lb % ROW_BLOCK == 0

    cc = jnp.concatenate([c, c_ctx[None], jnp.zeros((8 - batch - 1, d), f32)], axis=0)
    m = _adaln(cc, w_ada, b_ada).reshape(depth, 8, 3, 3, d)
    m_own = jnp.transpose(m[:, :batch], (0, 2, 1, 3, 4))
    m_ctx = jnp.broadcast_to(m[:, batch][:, :, None], (depth, 3, batch, 3, d))
    mods = jnp.concatenate([m_own, m_ctx, jnp.zeros((depth, 3, batch, 2, d), f32)], axis=3)
    norm_g4 = norm_g.reshape(depth, 3, 1, d)

    o_ba = 3 * ka + 2 * wa + 2 * kb + wb + wb
    o_ga = o_ba + 4 * heads
    w_bf = w_in.astype(bf16)
    w_cols = jnp.concatenate([w_bf[:, :, o_ga:], w_bf[:, :, :o_ba], w_bf[:, :, o_ba:o_ga],
                              jnp.zeros((depth, d, LANES - 4 * heads), bf16)], axis=2)
    base = 2 * d
    off_q, off_ff, off_fb, off_v, off_ag = base, base + ka, base + 2 * ka, base + 3 * ka, base + 3 * ka + wa
    off_bq = base + 3 * ka + 2 * wa
    off_bk, off_bv, off_bz = off_bq + kb, off_bq + 2 * kb, off_bq + 2 * kb + wb
    off_ab = base + o_ba

    wg, wu, wd = ffn_w_gate.astype(bf16), ffn_w_up.astype(bf16), ffn_w_down.astype(bf16)
    wbr_a, wbr_b, wo = w_branch_a.astype(bf16), w_branch_b.astype(bf16), w_out.astype(bf16)

    s = jnp.concatenate([ctx, x], axis=1).reshape(batch * lb, d)
    for l in range(depth):
        s = _ffn(s, norm_g4, mods, wg, wu, wd, l, 0, 0, lb, ctx_len)
        p = _proj(s, norm_g4, mods, w_cols, l, lb, ctx_len)
        oa = _hgrn(p, hgrn_lower_bounds, hgrn_norm_g, l, batch, lb, ctx_len, heads, off_q, off_ff, off_fb, off_v)
        ob = _dn(p, dn_conv_w, dn_a_log, dn_dt_bias, dn_norm_g, l, batch, lb, ctx_len, heads, off_bq, off_bk, off_bv,
                 off_ab)
        s = _merge(s, mods, oa, ob, p, wbr_a, wbr_b, wo, l, lb, ctx_len, off_ag, off_bz)
        s = _ffn(s, norm_g4, mods, wg, wu, wd, l, 2, 1, lb, ctx_len)
    return _final_norm(s.reshape(batch, lb, d), final_norm_g, ctx_len, seq)
```

```python
import functools

import jax
import jax.numpy as jnp
from jax import lax
from jax.experimental import pallas as pl
from jax.experimental.pallas import tpu as pltpu

EPS = 1e-6
FFN_RES = 0.5
HEAD_DIM = 128
CHUNK = 64
SUB = 8
ROW_BLOCK = 64
LANES = 128
VMEM_LIMIT_BYTES = 56 * 1024 * 1024

f32 = jnp.float32
bf16 = jnp.bfloat16


def _cparams(*sem):
    return pltpu.CompilerParams(dimension_semantics=sem, vmem_limit_bytes=VMEM_LIMIT_BYTES)


def _bf(a):
    return a.astype(bf16)


def _dot(a, b):
    return jnp.dot(_bf(a), _bf(b), preferred_element_type=f32)


def _dot_nt(a, b):
    return lax.dot_general(_bf(a), _bf(b), (((1,), (1,)), ((), ())), preferred_element_type=f32)


def _dot_tn(a, b):
    return lax.dot_general(_bf(a), _bf(b), (((0,), (0,)), ((), ())), preferred_element_type=f32)


def _split2(a):
    hi = _bf(a)
    lo = _bf(a - hi.astype(f32))
    return hi, lo


def _split3(a):
    hi = _bf(a)
    r = a - hi.astype(f32)
    mid = _bf(r)
    lo = _bf(r - mid.astype(f32))
    return hi, mid, lo


def _dot_sel(m01x3, a):
    return jnp.dot(m01x3, jnp.concatenate(_split3(a), axis=0), preferred_element_type=f32)


def _sigmoid(a):
    return 1.0 / (1.0 + jnp.exp(-a))


def _silu(a):
    return a * _sigmoid(a)


def _largest_tile(total, unit, cap):
    n = total // unit
    best = 1
    for k in range(1, n + 1):
        if n % k == 0 and k * unit <= cap:
            best = k
    return best * unit


def _adaln_kernel(c_ref, w_ref, b_ref, o_ref):
    a = _silu(c_ref[...])
    o_ref[...] = _dot(a, w_ref[...]) + b_ref[...]


def _adaln(cc, w_ada, b_ada):
    depth, d, n = w_ada.shape
    rows = cc.shape[0]
    tn = _largest_tile(n, LANES, 1024)
    return pl.pallas_call(
        _adaln_kernel,
        grid=(depth, n // tn),
        in_specs=[pl.BlockSpec((rows, d), lambda l, j: (0, 0)),
                  pl.BlockSpec((None, d, tn), lambda l, j: (l, 0, j)),
                  pl.BlockSpec((None, 1, tn), lambda l, j: (l, 0, j))],
        out_specs=pl.BlockSpec((None, rows, tn), lambda l, j: (l, 0, j)),
        out_shape=jax.ShapeDtypeStruct((depth, rows, n), f32),
        compiler_params=_cparams("parallel", "parallel"),
    )(cc, w_ada, b_ada.reshape(depth, 1, n))


def _is_ctx_rows(tile_idx, tiles_per_batch, tm, ctx_len):
    row0 = (tile_idx % tiles_per_batch) * tm
    rows = row0 + lax.broadcasted_iota(jnp.int32, (tm, 1), 0)
    return rows < ctx_len


def _row_block_loops(tm, tile_idx, tiles_per_batch, ctx_len, body):
    n_blk = tm // ROW_BLOCK
    row0 = (tile_idx % tiles_per_batch) * tm
    n_ctx = jnp.clip((ctx_len - row0) // ROW_BLOCK, 0, n_blk)
    for lo, hi, is_ctx in ((0, n_ctx, True), (n_ctx, n_blk, False)):
        def step(j, carry, is_ctx=is_ctx):
            body(pl.multiple_of(j * ROW_BLOCK, ROW_BLOCK), is_ctx)
            return carry
        lax.fori_loop(lo, hi, step, 0)


def _norm_mod_store(s_ref, g_ref, mod_ref, h_ref, tile_idx, tiles_per_batch, ctx_len):
    tm, d = s_ref.shape
    g = g_ref[...]
    mod = mod_ref[...]
    sub = ROW_BLOCK // 4
    gains = {True: jnp.broadcast_to(g * (1.0 + mod[4:5]), (sub, d)), False: jnp.broadcast_to(g * (1.0 + mod[1:2]), (sub, d))}
    shifts = {True: jnp.broadcast_to(mod[3:4], (sub, d)), False: jnp.broadcast_to(mod[0:1], (sub, d))}

    def body(r, is_ctx):
        for k in range(4):
            rk = r + k * sub
            x = s_ref[pl.ds(rk, sub), :]
            rs = lax.rsqrt(jnp.mean(x * x, axis=-1, keepdims=True) + EPS)
            h_ref[pl.ds(rk, sub), :] = _bf(x * rs * gains[is_ctx] + shifts[is_ctx])

    _row_block_loops(tm, tile_idx, tiles_per_batch, ctx_len, body)


def _residual_store(s_ref, mod_ref, o_ref, weight, tile_idx, tiles_per_batch, ctx_len):
    tm, d = s_ref.shape
    mod = mod_ref[...]
    gates = {True: jnp.broadcast_to(weight * mod[5:6], (ROW_BLOCK, d)), False: jnp.broadcast_to(weight * mod[2:3], (ROW_BLOCK, d))}

    def body(r, is_ctx):
        rows = pl.ds(r, ROW_BLOCK)
        o_ref[rows, :] = s_ref[rows, :] + gates[is_ctx] * o_ref[rows, :]

    _row_block_loops(tm, tile_idx, tiles_per_batch, ctx_len, body)


def _ffn_kernel(s_ref, g_ref, mod_ref, wg_ref, wu_ref, wd_ref, o_ref, h_ref, *, tiles_per_batch, ctx_len):
    i = pl.program_id(0)
    f = pl.program_id(1)

    @pl.when(f == 0)
    def _():
        _norm_mod_store(s_ref, g_ref, mod_ref, h_ref, i, tiles_per_batch, ctx_len)
        o_ref[...] = jnp.zeros_like(o_ref)

    h = h_ref[...]
    a = jnp.dot(h, wg_ref[...], preferred_element_type=f32)
    u = jnp.dot(h, wu_ref[...], preferred_element_type=f32)
    o_ref[...] += _dot(_silu(a) * u, wd_ref[...])

    @pl.when(f == pl.num_programs(1) - 1)
    def _():
        _residual_store(s_ref, mod_ref, o_ref, FFN_RES, i, tiles_per_batch, ctx_len)


def _ffn(s, g, mods, wg, wu, wd, layer, sub, which, lb, ctx_len):
    t, d = s.shape
    dff = wg.shape[-1]
    tm = _largest_tile(lb, SUB, 576)
    tf = _largest_tile(dff, LANES, 512)
    tpb = lb // tm
    kern = functools.partial(_ffn_kernel, tiles_per_batch=tpb, ctx_len=ctx_len)
    return pl.pallas_call(
        kern,
        grid=(t // tm, dff // tf),
        in_specs=[pl.BlockSpec((tm, d), lambda i, f: (i, 0)),
                  pl.BlockSpec((None, None, 1, d), lambda i, f: (layer, sub, 0, 0)),
                  pl.BlockSpec((None, None, None, 8, d), lambda i, f: (layer, sub, i // tpb, 0, 0)),
                  pl.BlockSpec((None, None, d, tf), lambda i, f: (layer, which, 0, f)),
                  pl.BlockSpec((None, None, d, tf), lambda i, f: (layer, which, 0, f)),
                  pl.BlockSpec((None, None, tf, d), lambda i, f: (layer, which, f, 0))],
        out_specs=pl.BlockSpec((tm, d), lambda i, f: (i, 0)),
        out_shape=jax.ShapeDtypeStruct((t, d), f32),
        scratch_shapes=[pltpu.VMEM((tm, d), bf16)],
        compiler_params=_cparams("parallel", "arbitrary"),
    )(s, g, mods, wg, wu, wd)


def _proj_kernel(s_ref, g_ref, mod_ref, w_ref, wab_ref, o_ref, oab_ref, h_ref, *, tiles_per_batch, ctx_len):
    @pl.when(pl.program_id(1) == 0)
    def _():
        _norm_mod_store(s_ref, g_ref, mod_ref, h_ref, pl.program_id(0), tiles_per_batch, ctx_len)
        oab_ref[...] = _dot_nt(h_ref[...], wab_ref[...])

    o_ref[...] = _dot_nt(h_ref[...], w_ref[...])


def _proj(s, g, mods, wt, layer, lb, ctx_len, n_lead):
    t, d = s.shape
    tm = _largest_tile(lb, SUB, 768)
    tn = LANES
    while tn * 2 <= 1024 and d % (tn * 2) == 0 and n_lead % (tn * 2) == 0:
        tn *= 2
    tpb = lb // tm
    n_gate = 2 * d // tn
    n_out = 2 * d + n_lead
    kern = functools.partial(_proj_kernel, tiles_per_batch=tpb, ctx_len=ctx_len)
    return pl.pallas_call(
        kern,
        grid=(t // tm, n_out // tn),
        in_specs=[pl.BlockSpec((tm, d), lambda i, j: (i, 0)),
                  pl.BlockSpec((None, None, 1, d), lambda i, j: (layer, 1, 0, 0)),
                  pl.BlockSpec((None, None, None, 8, d), lambda i, j: (layer, 1, i // tpb, 0, 0)),
                  pl.BlockSpec((None, tn, d), lambda i, j: (layer, jnp.where(j < n_gate, n_lead // tn + j, j - n_gate), 0)),
                  pl.BlockSpec((None, LANES, d), lambda i, j: (layer, n_out // LANES, 0))],
        out_specs=[pl.BlockSpec((tm, tn), lambda i, j: (i, j)),
                   pl.BlockSpec((tm, LANES), lambda i, j: (i, 0))],
        out_shape=[jax.ShapeDtypeStruct((t, n_out), f32), jax.ShapeDtypeStruct((t, LANES), f32)],
        scratch_shapes=[pltpu.VMEM((tm, d), bf16)],
        compiler_params=_cparams("parallel", "arbitrary"),
    )(s, g, mods, wt, wt)


def _chunk_index(i, n_ctx, n_chunks, reverse):
    if not reverse:
        return i
    return jnp.where(i < n_ctx, n_ctx - 1 - i, n_chunks + n_ctx - 1 - i)


def _group_size(n_chunks, n_ctx):
    for g in (4, 2):
        if n_chunks % g == 0 and n_ctx % g == 0:
            return g
    return 1


def _causal_masks(c, reverse):
    t = lax.broadcasted_iota(jnp.int32, (c, c), 0)
    s = lax.broadcasted_iota(jnp.int32, (c, c), 1)
    if reverse:
        return s >= t, s > t, t, s
    return s <= t, s < t, t, s


def _head_rms(o, g):
    return o * lax.rsqrt(jnp.mean(o * o, axis=-1, keepdims=True) + EPS) * g


def _hgrn_group(items, sts, consts):
    c = items[0]["q"].shape[0]
    sts = list(sts)
    for x in items:
        x["b"] = _dot_sel(consts[x["d"]][0], x["lc"])
    for x in items:
        d, b = x["d"], x["b"]
        b_end = b[0:1] if d else b[c - 1:c]
        x["o"] = _dot_nt(x["q"] * jnp.exp(b), sts[d])
        sts[d] = sts[d] * jnp.exp(b_end) + _dot_tn(x["v"], x["k"] * jnp.exp(b_end - b))

    for x in items:
        x["attn"] = jnp.zeros((c, c), f32)
    m, level = c // 2, 0
    while m >= SUB:
        n = c // m
        zeros_blk = jnp.zeros((m, LANES), f32)
        for x in items:
            d, b, q, k = x["d"], x["b"], x["q"], x["k"]
            qs, ks = [], []
            for i in range(n):
                lo, hi = i * m, (i + 1) * m
                if (i % 2 == 1) != bool(d):
                    edge = b[hi - 1:hi] if d else b[lo:lo + 1]
                    qs.append(q[lo:hi] * jnp.exp(jnp.minimum(b[lo:hi] - edge, 0.0)))
                    ks.append(zeros_blk)
                else:
                    edge = b[lo - 1:lo] if d else b[hi:hi + 1]
                    ks.append(k[lo:hi] * jnp.exp(jnp.minimum(edge - b[lo:hi], 0.0)))
                    qs.append(zeros_blk)
            x["lv"] = _dot_nt(jnp.concatenate(qs, axis=0), jnp.concatenate(ks, axis=0))
        for x in items:
            x["attn"] = x["attn"] + jnp.where(consts[x["d"]][1][level], x["lv"], 0.0)
        m, level = m // 2, level + 1
    for x in items:
        x["ov"] = _dot(x["attn"], x["v"])

    n = c // SUB
    for x in items:
        sub_mask, ones_red = consts[x["d"]][2:]
        step = SUB - 1 if x["d"] else 1
        q3 = x["q"].reshape(n, SUB, LANES)
        f3 = x["f"].reshape(n, SUB, LANES)
        u = x["k"].reshape(n, SUB, LANES)
        prods = [(q3 * u).reshape(c, LANES)]
        for j in range(1, SUB):
            u = f3 * pltpu.roll(u, step, 1)
            prods.append(jnp.where(sub_mask[j], q3 * u, 0.0).reshape(c, LANES))
        x["sums"] = jnp.dot(_bf(jnp.concatenate(prods, axis=0)), ones_red, preferred_element_type=f32)
    for x in items:
        step = SUB - 1 if x["d"] else 1
        w = x["v"].reshape(n, SUB, LANES)
        od = x["sums"][:c].reshape(n, SUB, LANES) * w
        for j in range(1, SUB):
            w = pltpu.roll(w, step, 1)
            od = od + x["sums"][j * c:(j + 1) * c].reshape(n, SUB, LANES) * w
        x["o"] = x["o"] + x["ov"] + od.reshape(c, LANES)
    return sts


def _hgrn_consts(c, reverse):
    m_incl, _, t, s = _causal_masks(c, reverse)
    level_masks = []
    m = c // 2
    while m >= SUB:
        tb, sb = t // m, s // m
        if reverse:
            level_masks.append((tb % 2 == 0) & (sb == tb + 1))
        else:
            level_masks.append((tb % 2 == 1) & (sb == tb - 1))
        m //= 2
    r = lax.broadcasted_iota(jnp.int32, (1, SUB, LANES), 1)
    sub_mask = [(r <= SUB - 1 - j) if reverse else (r >= j) for j in range(SUB)]
    row3 = lax.broadcasted_iota(jnp.int32, (c, 3 * c), 0)
    lane3 = lax.broadcasted_iota(jnp.int32, (c, 3 * c), 1) % c
    sel3 = jnp.where((lane3 >= row3) if reverse else (lane3 <= row3), 1.0, 0.0).astype(bf16)
    return (sel3, level_masks, sub_mask, jnp.ones((LANES, LANES), bf16))


def _hgrn_kernel(q_ref, ff_ref, fb_ref, v_ref, lb_ref, g_ref, o_ref, *, layer, ctx_len):
    c = CHUNK
    n_chunks = q_ref.shape[0] // c
    n_ctx = ctx_len // c

    raw = lb_ref[...]
    e = jnp.exp(raw - jnp.max(raw, axis=0, keepdims=True))
    sm = e / jnp.sum(e, axis=0, keepdims=True)
    lbs = jnp.sum(sm[:layer + 1], axis=0) - sm[0]

    consts = [_hgrn_consts(c, False), _hgrn_consts(c, True)]
    grp = _group_size(n_chunks, n_ctx)
    o_ref[...] = jnp.zeros_like(o_ref)

    def body(it, sts):
        items = []
        for j in range(grp):
            for d in (0, 1):
                r0 = pl.multiple_of(_chunk_index(it * grp + j, n_ctx, n_chunks, d == 1) * c, c)
                lb = lbs[d:d + 1]
                f = lb + (1.0 - lb) * _sigmoid((fb_ref if d else ff_ref)[pl.ds(r0, c), :])
                items.append(dict(d=d, r0=r0, q=_silu(q_ref[pl.ds(r0, c), :]), v=v_ref[pl.ds(r0, c), :],
                                  f=f, lc=jnp.log(f), k=1.0 - f))
        sts = _hgrn_group(items, sts, consts)
        for x in items:
            o_ref[pl.ds(x["r0"], c), :] += x["o"]
        return tuple(sts)

    zero_state = jnp.zeros((HEAD_DIM, HEAD_DIM), f32)
    lax.fori_loop(0, n_chunks // grp, body, (zero_state, zero_state))

    o_ref[...] = _head_rms(o_ref[...], g_ref[...])


def _hgrn(p, lower_bounds, norm_g, layer, batch, lb_rows, ctx_len, heads, off_q, off_ff, off_fb, off_v):
    t = p.shape[0]
    depth = lower_bounds.shape[0]

    def col(off):
        return pl.BlockSpec((lb_rows, HEAD_DIM), lambda b, h: (b, off // HEAD_DIM + h))

    kern = functools.partial(_hgrn_kernel, layer=layer, ctx_len=ctx_len)
    return pl.pallas_call(
        kern,
        grid=(batch, heads),
        in_specs=[col(off_q), col(off_ff), col(off_fb), col(off_v),
                  pl.BlockSpec((depth, 2, HEAD_DIM), lambda b, h: (0, 0, h)),
                  pl.BlockSpec((None, 1, HEAD_DIM), lambda b, h: (layer, 0, 0))],
        out_specs=pl.BlockSpec((lb_rows, HEAD_DIM), lambda b, h: (b, h)),
        out_shape=jax.ShapeDtypeStruct((t, heads * HEAD_DIM), f32),
        compiler_params=_cparams("parallel", "parallel"),
    )(p, p, p, p, lower_bounds, norm_g.reshape(depth, 1, HEAD_DIM))


def _short_conv_tile(u, cw, width, seq):
    r = u.shape[0]
    pos = lax.broadcasted_iota(jnp.int32, (r, 1), 0) % seq
    half = width // 2
    acc = u * cw[half:half + 1]
    for j in range(width):
        if j == half:
            continue
        uj = pltpu.roll(u, (half - j) % r, 0)
        valid = pos >= (half - j) if j < half else pos < seq - (j - half)
        acc = acc + jnp.where(valid, uj, 0.0) * cw[j:j + 1]
    return acc


def _dn_kernel(alog_ref, dt_ref, q_ref, k_ref, v_ref, ab_ref, cwq_ref, cwk_ref, cwv_ref, g_ref, o_ref,
               qn, kn, vn, mt_s, nt_s, qe_s, oe_s, gl_s, *, layer, ctx_len, heads):
    c = CHUNK
    h = pl.program_id(1)
    lb_rows = q_ref.shape[0]
    n_chunks = lb_rows // c
    n_ctx = ctx_len // c
    width = cwq_ref.shape[0]

    def prep(r0, rows, seq):
        for src, cw, dst, mode in ((q_ref, cwq_ref, qn, "q"), (k_ref, cwk_ref, kn, "k"), (v_ref, cwv_ref, vn, "v")):
            y = _silu(_short_conv_tile(src[pl.ds(r0, rows), :], cw[...], width, seq))
            if mode != "v":
                y = y * lax.rsqrt(jnp.sum(y * y, axis=-1, keepdims=True) + EPS)
            if mode == "q":
                y = y * (HEAD_DIM ** -0.5)
            dst[pl.ds(r0, rows), :] = y

    prep(0, ctx_len, ctx_len)
    rows_x = _largest_tile(lb_rows - ctx_len, c, 4 * c)

    def prep_body(i, carry):
        prep(pl.multiple_of(ctx_len + i * rows_x, c), rows_x, c)
        return carry

    lax.fori_loop(0, (lb_rows - ctx_len) // rows_x, prep_body, 0)

    sel_r = lax.broadcasted_iota(jnp.int32, (3 * LANES, 4 * LANES), 0) % LANES
    sel_c = lax.broadcasted_iota(jnp.int32, (3 * LANES, 4 * LANES), 1) // LANES
    ab_sel = jnp.where(sel_r == sel_c * heads + h, 1.0, 0.0).astype(bf16)

    row = lax.broadcasted_iota(jnp.int32, (c, LANES), 0)
    lane = lax.broadcasted_iota(jnp.int32, (c, LANES), 1)
    left_lanes = lane < c
    eye_right = jnp.where(lane == row + c, 1.0, 0.0)
    row3 = lax.broadcasted_iota(jnp.int32, (c, 3 * c), 0)
    lane3 = lax.broadcasted_iota(jnp.int32, (c, 3 * c), 1) % c
    ones3 = jnp.ones((c, 3 * c), bf16)
    zeros_k = jnp.zeros((c, HEAD_DIM), f32)
    dirs = []
    for d in (0, 1):
        if d == 0:
            m_incl, m_strict, m_incl_t, sel3 = lane <= row, lane < row, (row <= lane) & left_lanes, lane3 <= row3
        else:
            m_incl, m_strict, m_incl_t, sel3 = (lane >= row) & left_lanes, (lane > row) & left_lanes, lane <= row, lane3 >= row3
        neg_a = -jnp.exp(jnp.full((1, 1), alog_ref[layer, d, h], f32))
        dirs.append((m_incl, m_strict, m_incl_t, jnp.where(sel3, 1.0, 0.0).astype(bf16), neg_a, dt_ref[layer, d, h]))

    grp = _group_size(n_chunks, n_ctx)

    def pre_body(it, carry):
        chains = []
        for j in range(grp):
            ci = it * grp + j
            r0 = pl.multiple_of(ci * c, c)
            q = qn[pl.ds(r0, c), :]
            k = kn[pl.ds(r0, c), :]
            kq = _dot_nt(jnp.concatenate([k, q], axis=0), jnp.concatenate([k, zeros_k], axis=0))
            ab = jnp.dot(jnp.concatenate(_split3(ab_ref[pl.ds(r0, c), :]), axis=1), ab_sel,
                         preferred_element_type=f32)
            for d in (0, 1):
                chains.append(dict(d=d, ci=ci, r0=r0, q=q, k=k, kq=kq, a_in=ab[:, d * LANES:(d + 1) * LANES],
                                   b_in=ab[:, (2 + d) * LANES:(3 + d) * LANES]))
        for x in chains:
            _, _, m_incl_t, sel3, neg_a, dt = dirs[x["d"]]
            za = x["a_in"] + dt
            g = neg_a * (jnp.maximum(za, 0.0) + jnp.log(1.0 + jnp.exp(-jnp.abs(za))))
            x["beta"] = _sigmoid(x["b_in"])
            x["gc"] = _dot_sel(sel3, g)
            x["gc_row"] = _dot_sel(ones3, jnp.where(m_incl_t, g, 0.0))
        for x in chains:
            m_incl, m_strict = dirs[x["d"]][:2]
            x["gamma"] = jnp.where(m_incl, jnp.exp(jnp.minimum(x["gc"] - x["gc_row"], 0.0)), 0.0)
            x["x"] = eye_right - jnp.where(m_strict, x["beta"] * x["kq"][:c] * x["gamma"], 0.0)
        span = 1
        while span < c:
            for x in chains:
                xh, xl = _split2(x["x"])
                x["r"] = (jnp.dot(xh[:, :c], jnp.concatenate([xh, xl], axis=1), preferred_element_type=f32),
                          jnp.dot(xl[:, :c], xh, preferred_element_type=f32))
            for x in chains:
                r, r_lo = x["r"]
                x["x"] = r[:, :LANES] + r[:, LANES:] + r_lo + jnp.where(left_lanes, 0.0, x["x"])
            span *= 2
        for x in chains:
            r0 = x["r0"]
            x["egc"] = jnp.exp(x["gc"])
            kb = x["k"] * x["beta"]
            v = vn[pl.ds(r0, c), :]
            ah, al = _split2(x["x"][:, c:])
            bh, bl = _split2(jnp.concatenate([v * x["beta"], kb * x["egc"]], axis=1))
            r = jnp.dot(ah, jnp.concatenate([bh, bl], axis=1), preferred_element_type=f32)
            x["sol"] = r[:, :2 * HEAD_DIM] + r[:, 2 * HEAD_DIM:] + jnp.dot(al, bh, preferred_element_type=f32)
        for x in chains:
            gc = x["gc"]
            x["gc_end"] = gc[0:1] if x["d"] else gc[c - 1:c]
            x["kw"] = _dot_tn(x["k"] * jnp.exp(x["gc_end"] - gc), x["sol"])
            x["qw"] = _dot((x["kq"][c:] * x["gamma"])[:, :c], x["sol"])
        for x in chains:
            d, r0, ci = x["d"], x["r0"], x["ci"]
            nt_s[d, ci] = x["kw"][:, :HEAD_DIM]
            mt_s[d, ci] = -x["kw"][:, HEAD_DIM:]
            oe_s[d, pl.ds(r0, c), :] = x["qw"][:, :HEAD_DIM]
            qe_s[d, pl.ds(r0, c), :] = x["q"] * x["egc"] - x["qw"][:, HEAD_DIM:]
            gl_s[d, ci] = jnp.broadcast_to(x["gc_end"], (SUB, LANES))
        return carry

    lax.fori_loop(0, n_chunks // grp, pre_body, 0)

    o_ref[...] = jnp.zeros_like(o_ref)

    def scan_body(i, sts):
        cis = [_chunk_index(i, n_ctx, n_chunks, d == 1) for d in (0, 1)]
        r0s = [pl.multiple_of(ci * c, c) for ci in cis]
        lin = [_dot(jnp.concatenate([mt_s[d, cis[d]], qe_s[d, pl.ds(r0s[d], c), :]], axis=0), sts[d]) for d in (0, 1)]
        for d in (0, 1):
            o_ref[pl.ds(r0s[d], c), :] += lin[d][HEAD_DIM:] + oe_s[d, pl.ds(r0s[d], c), :]
        return tuple(jnp.exp(gl_s[d, cis[d]][0:1]) * sts[d] + lin[d][:HEAD_DIM] + nt_s[d, cis[d]] for d in (0, 1))

    zero_state = jnp.zeros((HEAD_DIM, HEAD_DIM), f32)
    lax.fori_loop(0, n_chunks, scan_body, (zero_state, zero_state))

    o_ref[...] = _head_rms(o_ref[...], g_ref[...])


def _dn(p, p_ab, conv_w, a_log, dt_bias, norm_g, layer, batch, lb_rows, ctx_len, heads, off_q, off_k, off_v):
    t = p.shape[0]
    depth, width, _ = conv_w.shape
    kb_cols = heads * HEAD_DIM
    n_chunks = lb_rows // CHUNK

    def col(off):
        return pl.BlockSpec((lb_rows, HEAD_DIM), lambda b, h: (b, off // HEAD_DIM + h))

    def cw(off):
        return pl.BlockSpec((None, width, HEAD_DIM), lambda b, h: (layer, 0, off // HEAD_DIM + h))

    smem = pl.BlockSpec(memory_space=pltpu.SMEM)
    kern = functools.partial(_dn_kernel, layer=layer, ctx_len=ctx_len, heads=heads)
    row_buf = pltpu.VMEM((lb_rows, HEAD_DIM), f32)
    dir_buf = pltpu.VMEM((2, lb_rows, HEAD_DIM), f32)
    state_buf = pltpu.VMEM((2, n_chunks, HEAD_DIM, HEAD_DIM), f32)
    return pl.pallas_call(
        kern,
        grid=(batch, heads),
        in_specs=[smem, smem, col(off_q), col(off_k), col(off_v),
                  pl.BlockSpec((lb_rows, LANES), lambda b, h: (b, 0)),
                  cw(0), cw(kb_cols), cw(2 * kb_cols),
                  pl.BlockSpec((None, 1, HEAD_DIM), lambda b, h: (layer, 0, 0))],
        out_specs=pl.BlockSpec((lb_rows, HEAD_DIM), lambda b, h: (b, h)),
        out_shape=jax.ShapeDtypeStruct((t, heads * HEAD_DIM), f32),
        scratch_shapes=[row_buf] * 3 + [state_buf] * 2 + [dir_buf] * 2 + [pltpu.VMEM((2, n_chunks, SUB, LANES), f32)],
        compiler_params=_cparams("parallel", "parallel"),
    )(a_log, dt_bias, p, p, p, p_ab, conv_w, conv_w, conv_w, norm_g.reshape(depth, 1, HEAD_DIM))


def _merge_kernel(s_ref, mod_ref, oa_ref, ob_ref, ag_ref, bz_ref, ga_ref, gb_ref, wa_ref, wb_ref, wo_ref, o_ref,
                  *, tiles_per_batch, ctx_len):
    is_ctx = _is_ctx_rows(pl.program_id(0), tiles_per_batch, s_ref.shape[0], ctx_len)
    ya = _bf(oa_ref[...] * _silu(ag_ref[...]))
    yb = _bf(ob_ref[...] * _silu(bz_ref[...]))
    y = (_sigmoid(ga_ref[...]) * jnp.dot(ya, wa_ref[...], preferred_element_type=f32)
         + _sigmoid(gb_ref[...]) * jnp.dot(yb, wb_ref[...], preferred_element_type=f32))
    mod = mod_ref[...]
    gate = jnp.where(is_ctx, mod[5:6], mod[2:3])
    o_ref[...] = s_ref[...] + gate * _dot(y, wo_ref[...])


def _merge(s, mods, oa, ob, p, wa, wb, wo, layer, lb, ctx_len, off_ag, off_bz):
    t, d = s.shape
    wa_cols = oa.shape[1]
    wb_cols = ob.shape[1]
    tm = _largest_tile(lb, SUB, 256)
    tpb = lb // tm
    kern = functools.partial(_merge_kernel, tiles_per_batch=tpb, ctx_len=ctx_len)

    def whole(a):
        return pl.BlockSpec((None,) + a.shape[1:], lambda i: (layer, 0, 0))

    return pl.pallas_call(
        kern,
        grid=(t // tm,),
        in_specs=[pl.BlockSpec((tm, d), lambda i: (i, 0)),
                  pl.BlockSpec((None, None, None, 8, d), lambda i: (layer, 1, i // tpb, 0, 0)),
                  pl.BlockSpec((tm, wa_cols), lambda i: (i, 0)),
                  pl.BlockSpec((tm, wb_cols), lambda i: (i, 0)),
                  pl.BlockSpec((tm, wa_cols), lambda i: (i, off_ag // wa_cols)),
                  pl.BlockSpec((tm, wb_cols), lambda i: (i, off_bz // wb_cols)),
                  pl.BlockSpec((tm, d), lambda i: (i, 0)),
                  pl.BlockSpec((tm, d), lambda i: (i, 1)),
                  whole(wa), whole(wb), whole(wo)],
        out_specs=pl.BlockSpec((tm, d), lambda i: (i, 0)),
        out_shape=jax.ShapeDtypeStruct((t, d), f32),
        compiler_params=_cparams("parallel"),
    )(s, mods, oa, ob, p, p, p, p, wa, wb, wo)


def _final_kernel(s_ref, g_ref, o_ref):
    s = s_ref[...]
    o_ref[...] = s * lax.rsqrt(jnp.mean(s * s, axis=-1, keepdims=True) + EPS) * g_ref[...]


def _final_norm(s3, g, ctx_len, seq):
    batch, _, d = s3.shape
    tm = _largest_tile(ctx_len, SUB, 256)
    return pl.pallas_call(
        _final_kernel,
        grid=(batch, seq // tm),
        in_specs=[pl.BlockSpec((None, tm, d), lambda b, j: (b, ctx_len // tm + j, 0)),
                  pl.BlockSpec((1, d), lambda b, j: (0, 0))],
        out_specs=pl.BlockSpec((None, tm, d), lambda b, j: (b, j, 0)),
        out_shape=jax.ShapeDtypeStruct((batch, seq, d), f32),
        compiler_params=_cparams("parallel", "parallel"),
    )(s3, g.reshape(1, d))


def kernel(x, c, ctx, c_ctx, w_ada, b_ada, norm_g, final_norm_g, ffn_w_gate, ffn_w_up, ffn_w_down, w_in,
           hgrn_lower_bounds, hgrn_norm_g, dn_conv_w, dn_a_log, dn_dt_bias, dn_norm_g, w_branch_a, w_branch_b, w_out):
    batch, seq, d = x.shape
    ctx_len = ctx.shape[1]
    depth = w_ada.shape[0]
    lb = ctx_len + seq
    ka = hgrn_lower_bounds.shape[-1]
    wa = w_branch_a.shape[1]
    wb = w_branch_b.shape[1]
    kb = (dn_conv_w.shape[-1] - wb) // 2
    heads = dn_a_log.shape[-1]
    assert ka == wa == kb == wb == heads * HEAD_DIM and seq % CHUNK == 0 and ctx_len % CHUNK == 0
    assert d % wa == 0 and 4 * heads <= LANES and batch + 1 <= 8 and ctx_len % ROW_BLOCK == 0 and lb % ROW_BLOCK == 0

    cc = jnp.concatenate([c, c_ctx[None], jnp.zeros((8 - batch - 1, d), f32)], axis=0)
    m = _adaln(cc, w_ada, b_ada).reshape(depth, 8, 3, 3, d)
    m_own = jnp.transpose(m[:, :batch], (0, 2, 1, 3, 4))
    m_ctx = jnp.broadcast_to(m[:, batch][:, :, None], (depth, 3, batch, 3, d))
    mods = jnp.concatenate([m_own, m_ctx, jnp.zeros((depth, 3, batch, 2, d), f32)], axis=3)
    norm_g4 = norm_g.reshape(depth, 3, 1, d)

    o_ba = 3 * ka + 2 * wa + 2 * kb + wb + wb
    wt = jnp.swapaxes(w_in, 1, 2).astype(bf16)
    wt_in = jnp.concatenate([wt[:, :o_ba], wt[:, o_ba + 4 * heads:], wt[:, o_ba:o_ba + 4 * heads],
                             jnp.zeros((depth, LANES - 4 * heads, d), bf16)], axis=1)
    base = 2 * d
    off_q, off_ff, off_fb, off_v, off_ag = base, base + ka, base + 2 * ka, base + 3 * ka, base + 3 * ka + wa
    off_bq = base + 3 * ka + 2 * wa
    off_bk, off_bv, off_bz = off_bq + kb, off_bq + 2 * kb, off_bq + 2 * kb + wb

    wg, wu, wd = ffn_w_gate.astype(bf16), ffn_w_up.astype(bf16), ffn_w_down.astype(bf16)
    wbr_a, wbr_b, wo = w_branch_a.astype(bf16), w_branch_b.astype(bf16), w_out.astype(bf16)

    s = jnp.concatenate([ctx, x], axis=1).reshape(batch * lb, d)
    for l in range(depth):
        s = _ffn(s, norm_g4, mods, wg, wu, wd, l, 0, 0, lb, ctx_len)
        p, p_ab = _proj(s, norm_g4, mods, wt_in, l, lb, ctx_len, o_ba)
        oa = _hgrn(p, hgrn_lower_bounds, hgrn_norm_g, l, batch, lb, ctx_len, heads, off_q, off_ff, off_fb, off_v)
        ob = _dn(p, p_ab, dn_conv_w, dn_a_log, dn_dt_bias, dn_norm_g, l, batch, lb, ctx_len, heads, off_bq, off_bk,
                 off_bv)
        s = _merge(s, mods, oa, ob, p, wbr_a, wbr_b, wo, l, lb, ctx_len, off_ag, off_bz)
        s = _ffn(s, norm_g4, mods, wg, wu, wd, l, 2, 1, lb, ctx_len)
    return _final_norm(s.reshape(batch, lb, d), final_norm_g, ctx_len, seq)
```

```python
import functools

import jax
import jax.numpy as jnp
from jax import lax
from jax.experimental import pallas as pl
from jax.experimental.pallas import tpu as pltpu

EPS = 1e-6
FFN_RES = 0.5
HEAD_DIM = 128
CHUNK = 64
SUB = 8
ROW_BLOCK = 64
LANES = 128
VMEM_LIMIT_BYTES = 56 * 1024 * 1024

f32 = jnp.float32
bf16 = jnp.bfloat16


def _cparams(*sem):
    return pltpu.CompilerParams(dimension_semantics=sem, vmem_limit_bytes=VMEM_LIMIT_BYTES)


def _bf(a):
    return a.astype(bf16)


def _dot(a, b):
    return jnp.dot(_bf(a), _bf(b), preferred_element_type=f32)


def _dot_nt(a, b):
    return lax.dot_general(_bf(a), _bf(b), (((1,), (1,)), ((), ())), preferred_element_type=f32)


def _dot_tn(a, b):
    return lax.dot_general(_bf(a), _bf(b), (((0,), (0,)), ((), ())), preferred_element_type=f32)


def _split2(a):
    hi = _bf(a)
    lo = _bf(a - hi.astype(f32))
    return hi, lo


def _split3(a):
    hi = _bf(a)
    r = a - hi.astype(f32)
    mid = _bf(r)
    lo = _bf(r - mid.astype(f32))
    return hi, mid, lo


def _dot_sel(m01x3, a):
    return jnp.dot(m01x3, jnp.concatenate(_split3(a), axis=0), preferred_element_type=f32)


def _sigmoid(a):
    return 1.0 / (1.0 + jnp.exp(-a))


def _silu(a):
    return a * _sigmoid(a)


def _largest_tile(total, unit, cap):
    n = total // unit
    best = 1
    for k in range(1, n + 1):
        if n % k == 0 and k * unit <= cap:
            best = k
    return best * unit


def _adaln_kernel(c_ref, w_ref, b_ref, o_ref):
    a = _silu(c_ref[...])
    o_ref[...] = _dot(a, w_ref[...]) + b_ref[...]


def _adaln(cc, w_ada, b_ada):
    depth, d, n = w_ada.shape
    rows = cc.shape[0]
    tn = _largest_tile(n, LANES, 1024)
    return pl.pallas_call(
        _adaln_kernel,
        grid=(depth, n // tn),
        in_specs=[pl.BlockSpec((rows, d), lambda l, j: (0, 0)),
                  pl.BlockSpec((None, d, tn), lambda l, j: (l, 0, j)),
                  pl.BlockSpec((None, 1, tn), lambda l, j: (l, 0, j))],
        out_specs=pl.BlockSpec((None, rows, tn), lambda l, j: (l, 0, j)),
        out_shape=jax.ShapeDtypeStruct((depth, rows, n), f32),
        compiler_params=_cparams("parallel", "parallel"),
    )(cc, w_ada, b_ada.reshape(depth, 1, n))


def _is_ctx_rows(tile_idx, tiles_per_batch, tm, ctx_len):
    row0 = (tile_idx % tiles_per_batch) * tm
    rows = row0 + lax.broadcasted_iota(jnp.int32, (tm, 1), 0)
    return rows < ctx_len


def _row_block_loops(tm, tile_idx, tiles_per_batch, ctx_len, body):
    n_blk = tm // ROW_BLOCK
    row0 = (tile_idx % tiles_per_batch) * tm
    n_ctx = jnp.clip((ctx_len - row0) // ROW_BLOCK, 0, n_blk)
    for lo, hi, is_ctx in ((0, n_ctx, True), (n_ctx, n_blk, False)):
        def step(j, carry, is_ctx=is_ctx):
            body(pl.multiple_of(j * ROW_BLOCK, ROW_BLOCK), is_ctx)
            return carry
        lax.fori_loop(lo, hi, step, 0)


def _norm_mod_store(s_ref, g_ref, mod_ref, h_ref, tile_idx, tiles_per_batch, ctx_len):
    tm, d = s_ref.shape
    g = g_ref[...]
    mod = mod_ref[...]
    sub = ROW_BLOCK // 4
    gains = {True: jnp.broadcast_to(g * (1.0 + mod[4:5]), (sub, d)), False: jnp.broadcast_to(g * (1.0 + mod[1:2]), (sub, d))}
    shifts = {True: jnp.broadcast_to(mod[3:4], (sub, d)), False: jnp.broadcast_to(mod[0:1], (sub, d))}

    def body(r, is_ctx):
        for k in range(4):
            rk = r + k * sub
            x = s_ref[pl.ds(rk, sub), :]
            rs = lax.rsqrt(jnp.mean(x * x, axis=-1, keepdims=True) + EPS)
            h_ref[pl.ds(rk, sub), :] = _bf(x * rs * gains[is_ctx] + shifts[is_ctx])

    _row_block_loops(tm, tile_idx, tiles_per_batch, ctx_len, body)


def _residual_store(s_ref, mod_ref, o_ref, weight, tile_idx, tiles_per_batch, ctx_len):
    tm, d = s_ref.shape
    mod = mod_ref[...]
    gates = {True: jnp.broadcast_to(weight * mod[5:6], (ROW_BLOCK, d)), False: jnp.broadcast_to(weight * mod[2:3], (ROW_BLOCK, d))}

    def body(r, is_ctx):
        rows = pl.ds(r, ROW_BLOCK)
        o_ref[rows, :] = s_ref[rows, :] + gates[is_ctx] * o_ref[rows, :]

    _row_block_loops(tm, tile_idx, tiles_per_batch, ctx_len, body)


def _ffn_kernel(s_ref, g_ref, mod_ref, wg_ref, wu_ref, wd_ref, o_ref, h_ref, *, tiles_per_batch, ctx_len):
    i = pl.program_id(0)
    f = pl.program_id(1)

    @pl.when(f == 0)
    def _():
        _norm_mod_store(s_ref, g_ref, mod_ref, h_ref, i, tiles_per_batch, ctx_len)
        o_ref[...] = jnp.zeros_like(o_ref)

    h = h_ref[...]
    a = jnp.dot(h, wg_ref[...], preferred_element_type=f32)
    u = jnp.dot(h, wu_ref[...], preferred_element_type=f32)
    o_ref[...] += _dot(_silu(a) * u, wd_ref[...])

    @pl.when(f == pl.num_programs(1) - 1)
    def _():
        _residual_store(s_ref, mod_ref, o_ref, FFN_RES, i, tiles_per_batch, ctx_len)


def _ffn(s, g, mods, wg, wu, wd, layer, sub, which, lb, ctx_len):
    t, d = s.shape
    dff = wg.shape[-1]
    tm = _largest_tile(lb, SUB, 576)
    tf = _largest_tile(dff, LANES, 512)
    tpb = lb // tm
    kern = functools.partial(_ffn_kernel, tiles_per_batch=tpb, ctx_len=ctx_len)
    return pl.pallas_call(
        kern,
        grid=(t // tm, dff // tf),
        in_specs=[pl.BlockSpec((tm, d), lambda i, f: (i, 0)),
                  pl.BlockSpec((None, None, 1, d), lambda i, f: (layer, sub, 0, 0)),
                  pl.BlockSpec((None, None, None, 8, d), lambda i, f: (layer, sub, i // tpb, 0, 0)),
                  pl.BlockSpec((None, None, d, tf), lambda i, f: (layer, which, 0, f)),
                  pl.BlockSpec((None, None, d, tf), lambda i, f: (layer, which, 0, f)),
                  pl.BlockSpec((None, None, tf, d), lambda i, f: (layer, which, f, 0))],
        out_specs=pl.BlockSpec((tm, d), lambda i, f: (i, 0)),
        out_shape=jax.ShapeDtypeStruct((t, d), f32),
        scratch_shapes=[pltpu.VMEM((tm, d), bf16)],
        compiler_params=_cparams("parallel", "arbitrary"),
    )(s, g, mods, wg, wu, wd)


def _proj_kernel(s_ref, g_ref, mod_ref, w_ref, wab_ref, o_ref, oab_ref, h_ref, *, tiles_per_batch, ctx_len):
    @pl.when(pl.program_id(1) == 0)
    def _():
        _norm_mod_store(s_ref, g_ref, mod_ref, h_ref, pl.program_id(0), tiles_per_batch, ctx_len)
        oab_ref[...] = _dot_nt(h_ref[...], wab_ref[...])

    o_ref[...] = _dot_nt(h_ref[...], w_ref[...])


def _proj(s, g, mods, wt, layer, lb, ctx_len, n_lead):
    t, d = s.shape
    tm = _largest_tile(lb, SUB, 768)
    tn = LANES
    while tn * 2 <= 1024 and d % (tn * 2) == 0 and n_lead % (tn * 2) == 0:
        tn *= 2
    tpb = lb // tm
    n_gate = 2 * d // tn
    n_out = 2 * d + n_lead
    kern = functools.partial(_proj_kernel, tiles_per_batch=tpb, ctx_len=ctx_len)
    return pl.pallas_call(
        kern,
        grid=(t // tm, n_out // tn),
        in_specs=[pl.BlockSpec((tm, d), lambda i, j: (i, 0)),
                  pl.BlockSpec((None, None, 1, d), lambda i, j: (layer, 1, 0, 0)),
                  pl.BlockSpec((None, None, None, 8, d), lambda i, j: (layer, 1, i // tpb, 0, 0)),
                  pl.BlockSpec((None, tn, d), lambda i, j: (layer, jnp.where(j < n_gate, n_lead // tn + j, j - n_gate), 0)),
                  pl.BlockSpec((None, LANES, d), lambda i, j: (layer, n_out // LANES, 0))],
        out_specs=[pl.BlockSpec((tm, tn), lambda i, j: (i, j)),
                   pl.BlockSpec((tm, LANES), lambda i, j: (i, 0))],
        out_shape=[jax.ShapeDtypeStruct((t, n_out), f32), jax.ShapeDtypeStruct((t, LANES), f32)],
        scratch_shapes=[pltpu.VMEM((tm, d), bf16)],
        compiler_params=_cparams("parallel", "arbitrary"),
    )(s, g, mods, wt, wt)


def _chunk_index(i, n_ctx, n_chunks, reverse):
    if not reverse:
        return i
    return jnp.where(i < n_ctx, n_ctx - 1 - i, n_chunks + n_ctx - 1 - i)


def _group_size(n_chunks, n_ctx):
    for g in (4, 2):
        if n_chunks % g == 0 and n_ctx % g == 0:
            return g
    return 1


def _causal_masks(c, reverse):
    t = lax.broadcasted_iota(jnp.int32, (c, c), 0)
    s = lax.broadcasted_iota(jnp.int32, (c, c), 1)
    if reverse:
        return s >= t, s > t, t, s
    return s <= t, s < t, t, s


def _head_rms(o, g):
    return o * lax.rsqrt(jnp.mean(o * o, axis=-1, keepdims=True) + EPS) * g


def _hgrn_group(items, sts, consts, between=()):
    c = items[0]["q"].shape[0]
    sts = list(sts)
    pending = list(between)

    def run_between():
        if pending:
            pending.pop(0)()

    for x in items:
        x["b"] = _dot_sel(consts[x["d"]][0], x["lc"])
    for x in items:
        d, b = x["d"], x["b"]
        b_end = b[0:1] if d else b[c - 1:c]
        x["o"] = _dot_nt(x["q"] * jnp.exp(b), sts[d])
        sts[d] = sts[d] * jnp.exp(b_end) + _dot_tn(x["v"], x["k"] * jnp.exp(b_end - b))
    run_between()

    for x in items:
        x["attn"] = jnp.zeros((c, c), f32)
    m, level = c // 2, 0
    while m >= SUB:
        n = c // m
        zeros_blk = jnp.zeros((m, LANES), f32)
        for x in items:
            d, b, q, k = x["d"], x["b"], x["q"], x["k"]
            qs, ks = [], []
            for i in range(n):
                lo, hi = i * m, (i + 1) * m
                if (i % 2 == 1) != bool(d):
                    edge = b[hi - 1:hi] if d else b[lo:lo + 1]
                    qs.append(q[lo:hi] * jnp.exp(jnp.minimum(b[lo:hi] - edge, 0.0)))
                    ks.append(zeros_blk)
                else:
                    edge = b[lo - 1:lo] if d else b[hi:hi + 1]
                    ks.append(k[lo:hi] * jnp.exp(jnp.minimum(edge - b[lo:hi], 0.0)))
                    qs.append(zeros_blk)
            x["lv"] = _dot_nt(jnp.concatenate(qs, axis=0), jnp.concatenate(ks, axis=0))
        for x in items:
            x["attn"] = x["attn"] + jnp.where(consts[x["d"]][1][level], x["lv"], 0.0)
        m, level = m // 2, level + 1
        run_between()
    for x in items:
        x["ov"] = _dot(x["attn"], x["v"])

    n = c // SUB
    for x in items:
        sub_mask, ones_red = consts[x["d"]][2:]
        step = SUB - 1 if x["d"] else 1
        q3 = x["q"].reshape(n, SUB, LANES)
        f3 = x["f"].reshape(n, SUB, LANES)
        u = x["k"].reshape(n, SUB, LANES)
        prods = [(q3 * u).reshape(c, LANES)]
        for j in range(1, SUB):
            u = f3 * pltpu.roll(u, step, 1)
            prods.append(jnp.where(sub_mask[j], q3 * u, 0.0).reshape(c, LANES))
        x["sums"] = jnp.dot(_bf(jnp.concatenate(prods, axis=0)), ones_red, preferred_element_type=f32)
    while pending:
        run_between()
    for x in items:
        step = SUB - 1 if x["d"] else 1
        w = x["v"].reshape(n, SUB, LANES)
        od = x["sums"][:c].reshape(n, SUB, LANES) * w
        for j in range(1, SUB):
            w = pltpu.roll(w, step, 1)
            od = od + x["sums"][j * c:(j + 1) * c].reshape(n, SUB, LANES) * w
        x["o"] = x["o"] + x["ov"] + od.reshape(c, LANES)
    return sts


def _hgrn_consts(c, reverse):
    m_incl, _, t, s = _causal_masks(c, reverse)
    level_masks = []
    m = c // 2
    while m >= SUB:
        tb, sb = t // m, s // m
        if reverse:
            level_masks.append((tb % 2 == 0) & (sb == tb + 1))
        else:
            level_masks.append((tb % 2 == 1) & (sb == tb - 1))
        m //= 2
    r = lax.broadcasted_iota(jnp.int32, (1, SUB, LANES), 1)
    sub_mask = [(r <= SUB - 1 - j) if reverse else (r >= j) for j in range(SUB)]
    row3 = lax.broadcasted_iota(jnp.int32, (c, 3 * c), 0)
    lane3 = lax.broadcasted_iota(jnp.int32, (c, 3 * c), 1) % c
    sel3 = jnp.where((lane3 >= row3) if reverse else (lane3 <= row3), 1.0, 0.0).astype(bf16)
    return (sel3, level_masks, sub_mask, jnp.ones((LANES, LANES), bf16))


def _short_conv_tile(u, cw, width, seq):
    r = u.shape[0]
    pos = lax.broadcasted_iota(jnp.int32, (r, 1), 0) % seq
    half = width // 2
    acc = u * cw[half:half + 1]
    for j in range(width):
        if j == half:
            continue
        uj = pltpu.roll(u, (half - j) % r, 0)
        valid = pos >= (half - j) if j < half else pos < seq - (j - half)
        acc = acc + jnp.where(valid, uj, 0.0) * cw[j:j + 1]
    return acc


def _mixers_kernel(alog_ref, dt_ref, hq_ref, hff_ref, hfb_ref, hv_ref, lb_ref, hg_ref,
                   q_ref, k_ref, v_ref, ab_ref, cwq_ref, cwk_ref, cwv_ref, g_ref, oa_ref, o_ref,
                   qn, kn, vn, mt_s, nt_s, qe_s, oe_s, gl_s, *, layer, ctx_len, heads):
    c = CHUNK
    h = pl.program_id(1)
    lb_rows = q_ref.shape[0]
    n_chunks = lb_rows // c
    n_ctx = ctx_len // c
    width = cwq_ref.shape[0]

    def prep(r0, rows, seq):
        for src, cw, dst, mode in ((q_ref, cwq_ref, qn, "q"), (k_ref, cwk_ref, kn, "k"), (v_ref, cwv_ref, vn, "v")):
            y = _silu(_short_conv_tile(src[pl.ds(r0, rows), :], cw[...], width, seq))
            if mode != "v":
                y = y * lax.rsqrt(jnp.sum(y * y, axis=-1, keepdims=True) + EPS)
            if mode == "q":
                y = y * (HEAD_DIM ** -0.5)
            dst[pl.ds(r0, rows), :] = y

    prep(0, ctx_len, ctx_len)
    rows_x = _largest_tile(lb_rows - ctx_len, c, 4 * c)

    def prep_body(i, carry):
        prep(pl.multiple_of(ctx_len + i * rows_x, c), rows_x, c)
        return carry

    lax.fori_loop(0, (lb_rows - ctx_len) // rows_x, prep_body, 0)

    sel_r = lax.broadcasted_iota(jnp.int32, (3 * LANES, 4 * LANES), 0) % LANES
    sel_c = lax.broadcasted_iota(jnp.int32, (3 * LANES, 4 * LANES), 1) // LANES
    ab_sel = jnp.where(sel_r == sel_c * heads + h, 1.0, 0.0).astype(bf16)

    row = lax.broadcasted_iota(jnp.int32, (c, LANES), 0)
    lane = lax.broadcasted_iota(jnp.int32, (c, LANES), 1)
    left_lanes = lane < c
    eye_right = jnp.where(lane == row + c, 1.0, 0.0)
    row3 = lax.broadcasted_iota(jnp.int32, (c, 3 * c), 0)
    lane3 = lax.broadcasted_iota(jnp.int32, (c, 3 * c), 1) % c
    ones3 = jnp.ones((c, 3 * c), bf16)
    zeros_k = jnp.zeros((c, HEAD_DIM), f32)
    dirs = []
    for d in (0, 1):
        if d == 0:
            m_incl, m_strict, m_incl_t, sel3 = lane <= row, lane < row, (row <= lane) & left_lanes, lane3 <= row3
        else:
            m_incl, m_strict, m_incl_t, sel3 = (lane >= row) & left_lanes, (lane > row) & left_lanes, lane <= row, lane3 >= row3
        neg_a = -jnp.exp(jnp.full((1, 1), alog_ref[layer, d, h], f32))
        dirs.append((m_incl, m_strict, m_incl_t, jnp.where(sel3, 1.0, 0.0).astype(bf16), neg_a, dt_ref[layer, d, h]))

    grp = max(g for g in (12, 6, 4, 3, 2, 1) if n_chunks % g == 0)

    def pre_body(it, carry):
        chains = []
        for j in range(grp):
            ci = it * grp + j
            r0 = pl.multiple_of(ci * c, c)
            q = qn[pl.ds(r0, c), :]
            k = kn[pl.ds(r0, c), :]
            kq = _dot_nt(jnp.concatenate([k, q], axis=0), jnp.concatenate([k, zeros_k], axis=0))
            ab = jnp.dot(jnp.concatenate(_split3(ab_ref[pl.ds(r0, c), :]), axis=1), ab_sel,
                         preferred_element_type=f32)
            for d in (0, 1):
                chains.append(dict(d=d, ci=ci, r0=r0, q=q, k=k, kq=kq, a_in=ab[:, d * LANES:(d + 1) * LANES],
                                   b_in=ab[:, (2 + d) * LANES:(3 + d) * LANES]))
        for x in chains:
            _, _, m_incl_t, sel3, neg_a, dt = dirs[x["d"]]
            za = x["a_in"] + dt
            g = neg_a * (jnp.maximum(za, 0.0) + jnp.log(1.0 + jnp.exp(-jnp.abs(za))))
            x["beta"] = _sigmoid(x["b_in"])
            x["gc"] = _dot_sel(sel3, g)
            x["gc_row"] = _dot_sel(ones3, jnp.where(m_incl_t, g, 0.0))
        for x in chains:
            m_incl, m_strict = dirs[x["d"]][:2]
            x["gamma"] = jnp.where(m_incl, jnp.exp(jnp.minimum(x["gc"] - x["gc_row"], 0.0)), 0.0)
            x["x"] = eye_right - jnp.where(m_strict, x["beta"] * x["kq"][:c] * x["gamma"], 0.0)
        span = 1
        while span < c:
            for x in chains:
                xh, xl = _split2(x["x"])
                x["r"] = (jnp.dot(xh[:, :c], jnp.concatenate([xh, xl], axis=1), preferred_element_type=f32),
                          jnp.dot(xl[:, :c], xh, preferred_element_type=f32))
            for x in chains:
                r, r_lo = x["r"]
                x["x"] = r[:, :LANES] + r[:, LANES:] + r_lo + jnp.where(left_lanes, 0.0, x["x"])
            span *= 2
        for x in chains:
            r0 = x["r0"]
            x["egc"] = jnp.exp(x["gc"])
            kb = x["k"] * x["beta"]
            v = vn[pl.ds(r0, c), :]
            ah, al = _split2(x["x"][:, c:])
            bh, bl = _split2(jnp.concatenate([v * x["beta"], kb * x["egc"]], axis=1))
            r = jnp.dot(ah, jnp.concatenate([bh, bl], axis=1), preferred_element_type=f32)
            x["sol"] = r[:, :2 * HEAD_DIM] + r[:, 2 * HEAD_DIM:] + jnp.dot(al, bh, preferred_element_type=f32)
        for x in chains:
            gc = x["gc"]
            x["gc_end"] = gc[0:1] if x["d"] else gc[c - 1:c]
            x["kw"] = _dot_tn(x["k"] * jnp.exp(x["gc_end"] - gc), x["sol"])
            x["qw"] = _dot((x["kq"][c:] * x["gamma"])[:, :c], x["sol"])
        for x in chains:
            d, r0, ci = x["d"], x["r0"], x["ci"]
            nt_s[d, ci] = x["kw"][:, :HEAD_DIM]
            mt_s[d, ci] = -x["kw"][:, HEAD_DIM:]
            oe_s[d, pl.ds(r0, c), :] = x["qw"][:, :HEAD_DIM]
            qe_s[d, pl.ds(r0, c), :] = x["q"] * x["egc"] - x["qw"][:, HEAD_DIM:]
            gl_s[d, ci] = jnp.broadcast_to(x["gc_end"], (SUB, LANES))
        return carry

    lax.fori_loop(0, n_chunks // grp, pre_body, 0)

    o_ref[...] = jnp.zeros_like(o_ref)
    oa_ref[...] = jnp.zeros_like(oa_ref)

    def dn_step(i, sts):
        cis = [_chunk_index(i, n_ctx, n_chunks, d == 1) for d in (0, 1)]
        r0s = [pl.multiple_of(ci * c, c) for ci in cis]
        lin = [_dot(jnp.concatenate([mt_s[d, cis[d]], qe_s[d, pl.ds(r0s[d], c), :]], axis=0), sts[d]) for d in (0, 1)]
        for d in (0, 1):
            o_ref[pl.ds(r0s[d], c), :] += lin[d][HEAD_DIM:] + oe_s[d, pl.ds(r0s[d], c), :]
        return [jnp.exp(gl_s[d, cis[d]][0:1]) * sts[d] + lin[d][:HEAD_DIM] + nt_s[d, cis[d]] for d in (0, 1)]

    raw = lb_ref[...]
    e = jnp.exp(raw - jnp.max(raw, axis=0, keepdims=True))
    sm = e / jnp.sum(e, axis=0, keepdims=True)
    lbs = jnp.sum(sm[:layer + 1], axis=0) - sm[0]
    consts = [_hgrn_consts(c, False), _hgrn_consts(c, True)]
    hgrp = _group_size(n_chunks, n_ctx)

    def body(it, carry):
        dn_sts = list(carry[2:])
        items = []
        for j in range(hgrp):
            for d in (0, 1):
                r0 = pl.multiple_of(_chunk_index(it * hgrp + j, n_ctx, n_chunks, d == 1) * c, c)
                lb = lbs[d:d + 1]
                f = lb + (1.0 - lb) * _sigmoid((hfb_ref if d else hff_ref)[pl.ds(r0, c), :])
                items.append(dict(d=d, r0=r0, q=_silu(hq_ref[pl.ds(r0, c), :]), v=hv_ref[pl.ds(r0, c), :],
                                  f=f, lc=jnp.log(f), k=1.0 - f))

        def scan_step(j):
            def run():
                dn_sts[:] = dn_step(it * hgrp + j, dn_sts)
            return run

        sts = _hgrn_group(items, carry[:2], consts, between=[scan_step(j) for j in range(hgrp)])
        for x in items:
            oa_ref[pl.ds(x["r0"], c), :] += x["o"]
        return (*sts, *dn_sts)

    zero_state = jnp.zeros((HEAD_DIM, HEAD_DIM), f32)
    lax.fori_loop(0, n_chunks // hgrp, body, (zero_state,) * 4)

    o_ref[...] = _head_rms(o_ref[...], g_ref[...])
    oa_ref[...] = _head_rms(oa_ref[...], hg_ref[...])


def _mixers(p, p_ab, lower_bounds, hgrn_norm_g, conv_w, a_log, dt_bias, dn_norm_g, layer, batch, lb_rows, ctx_len, heads,
            off_hq, off_hff, off_hfb, off_hv, off_q, off_k, off_v):
    t = p.shape[0]
    depth, width, _ = conv_w.shape
    kb_cols = heads * HEAD_DIM
    n_chunks = lb_rows // CHUNK

    def col(off):
        return pl.BlockSpec((lb_rows, HEAD_DIM), lambda b, h: (b, off // HEAD_DIM + h))

    def cw(off):
        return pl.BlockSpec((None, width, HEAD_DIM), lambda b, h: (layer, 0, off // HEAD_DIM + h))

    def gain():
        return pl.BlockSpec((None, 1, HEAD_DIM), lambda b, h: (layer, 0, 0))

    smem = pl.BlockSpec(memory_space=pltpu.SMEM)
    kern = functools.partial(_mixers_kernel, layer=layer, ctx_len=ctx_len, heads=heads)
    row_buf = pltpu.VMEM((lb_rows, HEAD_DIM), f32)
    dir_buf = pltpu.VMEM((2, lb_rows, HEAD_DIM), f32)
    state_buf = pltpu.VMEM((2, n_chunks, HEAD_DIM, HEAD_DIM), f32)
    out = jax.ShapeDtypeStruct((t, heads * HEAD_DIM), f32)
    return pl.pallas_call(
        kern,
        grid=(batch, heads),
        in_specs=[smem, smem, col(off_hq), col(off_hff), col(off_hfb), col(off_hv),
                  pl.BlockSpec((depth, 2, HEAD_DIM), lambda b, h: (0, 0, h)), gain(),
                  col(off_q), col(off_k), col(off_v),
                  pl.BlockSpec((lb_rows, LANES), lambda b, h: (b, 0)),
                  cw(0), cw(kb_cols), cw(2 * kb_cols), gain()],
        out_specs=[pl.BlockSpec((lb_rows, HEAD_DIM), lambda b, h: (b, h))] * 2,
        out_shape=[out, out],
        scratch_shapes=[row_buf] * 3 + [state_buf] * 2 + [dir_buf] * 2 + [pltpu.VMEM((2, n_chunks, SUB, LANES), f32)],
        compiler_params=_cparams("parallel", "parallel"),
    )(a_log, dt_bias, p, p, p, p, lower_bounds, hgrn_norm_g.reshape(depth, 1, HEAD_DIM),
      p, p, p, p_ab, conv_w, conv_w, conv_w, dn_norm_g.reshape(depth, 1, HEAD_DIM))


def _merge_kernel(s_ref, mod_ref, oa_ref, ob_ref, ag_ref, bz_ref, ga_ref, gb_ref, wa_ref, wb_ref, wo_ref, o_ref,
                  *, tiles_per_batch, ctx_len):
    is_ctx = _is_ctx_rows(pl.program_id(0), tiles_per_batch, s_ref.shape[0], ctx_len)
    ya = _bf(oa_ref[...] * _silu(ag_ref[...]))
    yb = _bf(ob_ref[...] * _silu(bz_ref[...]))
    y = (_sigmoid(ga_ref[...]) * jnp.dot(ya, wa_ref[...], preferred_element_type=f32)
         + _sigmoid(gb_ref[...]) * jnp.dot(yb, wb_ref[...], preferred_element_type=f32))
    mod = mod_ref[...]
    gate = jnp.where(is_ctx, mod[5:6], mod[2:3])
    o_ref[...] = s_ref[...] + gate * _dot(y, wo_ref[...])


def _merge(s, mods, oa, ob, p, wa, wb, wo, layer, lb, ctx_len, off_ag, off_bz):
    t, d = s.shape
    wa_cols = oa.shape[1]
    wb_cols = ob.shape[1]
    tm = _largest_tile(lb, SUB, 256)
    tpb = lb // tm
    kern = functools.partial(_merge_kernel, tiles_per_batch=tpb, ctx_len=ctx_len)

    def whole(a):
        return pl.BlockSpec((None,) + a.shape[1:], lambda i: (layer, 0, 0))

    return pl.pallas_call(
        kern,
        grid=(t // tm,),
        in_specs=[pl.BlockSpec((tm, d), lambda i: (i, 0)),
                  pl.BlockSpec((None, None, None, 8, d), lambda i: (layer, 1, i // tpb, 0, 0)),
                  pl.BlockSpec((tm, wa_cols), lambda i: (i, 0)),
                  pl.BlockSpec((tm, wb_cols), lambda i: (i, 0)),
                  pl.BlockSpec((tm, wa_cols), lambda i: (i, off_ag // wa_cols)),
                  pl.BlockSpec((tm, wb_cols), lambda i: (i, off_bz // wb_cols)),
                  pl.BlockSpec((tm, d), lambda i: (i, 0)),
                  pl.BlockSpec((tm, d), lambda i: (i, 1)),
                  whole(wa), whole(wb), whole(wo)],
        out_specs=pl.BlockSpec((tm, d), lambda i: (i, 0)),
        out_shape=jax.ShapeDtypeStruct((t, d), f32),
        compiler_params=_cparams("parallel"),
    )(s, mods, oa, ob, p, p, p, p, wa, wb, wo)


def _final_kernel(s_ref, g_ref, o_ref):
    s = s_ref[...]
    o_ref[...] = s * lax.rsqrt(jnp.mean(s * s, axis=-1, keepdims=True) + EPS) * g_ref[...]


def _final_norm(s3, g, ctx_len, seq):
    batch, _, d = s3.shape
    tm = _largest_tile(ctx_len, SUB, 256)
    return pl.pallas_call(
        _final_kernel,
        grid=(batch, seq // tm),
        in_specs=[pl.BlockSpec((None, tm, d), lambda b, j: (b, ctx_len // tm + j, 0)),
                  pl.BlockSpec((1, d), lambda b, j: (0, 0))],
        out_specs=pl.BlockSpec((None, tm, d), lambda b, j: (b, j, 0)),
        out_shape=jax.ShapeDtypeStruct((batch, seq, d), f32),
        compiler_params=_cparams("parallel", "parallel"),
    )(s3, g.reshape(1, d))


def kernel(x, c, ctx, c_ctx, w_ada, b_ada, norm_g, final_norm_g, ffn_w_gate, ffn_w_up, ffn_w_down, w_in,
           hgrn_lower_bounds, hgrn_norm_g, dn_conv_w, dn_a_log, dn_dt_bias, dn_norm_g, w_branch_a, w_branch_b, w_out):
    batch, seq, d = x.shape
    ctx_len = ctx.shape[1]
    depth = w_ada.shape[0]
    lb = ctx_len + seq
    ka = hgrn_lower_bounds.shape[-1]
    wa = w_branch_a.shape[1]
    wb = w_branch_b.shape[1]
    kb = (dn_conv_w.shape[-1] - wb) // 2
    heads = dn_a_log.shape[-1]
    assert ka == wa == kb == wb == heads * HEAD_DIM and seq % CHUNK == 0 and ctx_len % CHUNK == 0
    assert d % wa == 0 and 4 * heads <= LANES and batch + 1 <= 8 and ctx_len % ROW_BLOCK == 0 and lb % ROW_BLOCK == 0

    cc = jnp.concatenate([c, c_ctx[None], jnp.zeros((8 - batch - 1, d), f32)], axis=0)
    m = _adaln(cc, w_ada, b_ada).reshape(depth, 8, 3, 3, d)
    m_own = jnp.transpose(m[:, :batch], (0, 2, 1, 3, 4))
    m_ctx = jnp.broadcast_to(m[:, batch][:, :, None], (depth, 3, batch, 3, d))
    mods = jnp.concatenate([m_own, m_ctx, jnp.zeros((depth, 3, batch, 2, d), f32)], axis=3)
    norm_g4 = norm_g.reshape(depth, 3, 1, d)

    o_ba = 3 * ka + 2 * wa + 2 * kb + wb + wb
    wt = jnp.swapaxes(w_in, 1, 2).astype(bf16)
    wt_in = jnp.concatenate([wt[:, :o_ba], wt[:, o_ba + 4 * heads:], wt[:, o_ba:o_ba + 4 * heads],
                             jnp.zeros((depth, LANES - 4 * heads, d), bf16)], axis=1)
    base = 2 * d
    off_q, off_ff, off_fb, off_v, off_ag = base, base + ka, base + 2 * ka, base + 3 * ka, base + 3 * ka + wa
    off_bq = base + 3 * ka + 2 * wa
    off_bk, off_bv, off_bz = off_bq + kb, off_bq + 2 * kb, off_bq + 2 * kb + wb

    wg, wu, wd = ffn_w_gate.astype(bf16), ffn_w_up.astype(bf16), ffn_w_down.astype(bf16)
    wbr_a, wbr_b, wo = w_branch_a.astype(bf16), w_branch_b.astype(bf16), w_out.astype(bf16)

    s = jnp.concatenate([ctx, x], axis=1).reshape(batch * lb, d)
    for l in range(depth):
        s = _ffn(s, norm_g4, mods, wg, wu, wd, l, 0, 0, lb, ctx_len)
        p, p_ab = _proj(s, norm_g4, mods, wt_in, l, lb, ctx_len, o_ba)
        oa, ob = _mixers(p, p_ab, hgrn_lower_bounds, hgrn_norm_g, dn_conv_w, dn_a_log, dn_dt_bias, dn_norm_g, l, batch, lb,
                         ctx_len, heads, off_q, off_ff, off_fb, off_v, off_bq, off_bk, off_bv)
        s = _merge(s, mods, oa, ob, p, wbr_a, wbr_b, wo, l, lb, ctx_len, off_ag, off_bz)
        s = _ffn(s, norm_g4, mods, wg, wu, wd, l, 2, 1, lb, ctx_len)
    return _final_norm(s.reshape(batch, lb, d), final_norm_g, ctx_len, seq)
```

```python
import functools

import jax
import jax.numpy as jnp
from jax import lax
from jax.experimental import pallas as pl
from jax.experimental.pallas import tpu as pltpu

EPS = 1e-6
FFN_RES = 0.5
HEAD_DIM = 128
CHUNK = 64
SUB = 8
ROW_BLOCK = 64
LANES = 128
VMEM_LIMIT_BYTES = 56 * 1024 * 1024

f32 = jnp.float32
bf16 = jnp.bfloat16


def _cparams(*sem):
    return pltpu.CompilerParams(dimension_semantics=sem, vmem_limit_bytes=VMEM_LIMIT_BYTES)


def _bf(a):
    return a.astype(bf16)


def _dot(a, b):
    return jnp.dot(_bf(a), _bf(b), preferred_element_type=f32)


def _dot_nt(a, b):
    return lax.dot_general(_bf(a), _bf(b), (((1,), (1,)), ((), ())), preferred_element_type=f32)


def _dot_tn(a, b):
    return lax.dot_general(_bf(a), _bf(b), (((0,), (0,)), ((), ())), preferred_element_type=f32)


def _split2(a):
    hi = _bf(a)
    lo = _bf(a - hi.astype(f32))
    return hi, lo


def _split3(a):
    hi = _bf(a)
    r = a - hi.astype(f32)
    mid = _bf(r)
    lo = _bf(r - mid.astype(f32))
    return hi, mid, lo


def _dot_sel(m01x3, a):
    return jnp.dot(m01x3, jnp.concatenate(_split3(a), axis=0), preferred_element_type=f32)


def _sigmoid(a):
    return 1.0 / (1.0 + jnp.exp(-a))


def _silu(a):
    return a * _sigmoid(a)


def _largest_tile(total, unit, cap):
    n = total // unit
    best = 1
    for k in range(1, n + 1):
        if n % k == 0 and k * unit <= cap:
            best = k
    return best * unit


def _adaln_kernel(c_ref, w_ref, b_ref, o_ref):
    a = _silu(c_ref[...])
    o_ref[...] = _dot(a, w_ref[...]) + b_ref[...]


def _adaln(cc, w_ada, b_ada):
    depth, d, n = w_ada.shape
    rows = cc.shape[0]
    tn = _largest_tile(n, LANES, 1024)
    return pl.pallas_call(
        _adaln_kernel,
        grid=(depth, n // tn),
        in_specs=[pl.BlockSpec((rows, d), lambda l, j: (0, 0)),
                  pl.BlockSpec((None, d, tn), lambda l, j: (l, 0, j)),
                  pl.BlockSpec((None, 1, tn), lambda l, j: (l, 0, j))],
        out_specs=pl.BlockSpec((None, rows, tn), lambda l, j: (l, 0, j)),
        out_shape=jax.ShapeDtypeStruct((depth, rows, n), f32),
        compiler_params=_cparams("parallel", "parallel"),
    )(cc, w_ada, b_ada.reshape(depth, 1, n))


def _is_ctx_rows(tile_idx, tiles_per_batch, tm, ctx_len):
    row0 = (tile_idx % tiles_per_batch) * tm
    rows = row0 + lax.broadcasted_iota(jnp.int32, (tm, 1), 0)
    return rows < ctx_len


def _row_block_loops(tm, tile_idx, tiles_per_batch, ctx_len, body):
    n_blk = tm // ROW_BLOCK
    row0 = (tile_idx % tiles_per_batch) * tm
    n_ctx = jnp.clip((ctx_len - row0) // ROW_BLOCK, 0, n_blk)
    for lo, hi, is_ctx in ((0, n_ctx, True), (n_ctx, n_blk, False)):
        def step(j, carry, is_ctx=is_ctx):
            body(pl.multiple_of(j * ROW_BLOCK, ROW_BLOCK), is_ctx)
            return carry
        lax.fori_loop(lo, hi, step, 0)


def _norm_mod_store(s_ref, g_ref, mod_ref, h_ref, tile_idx, tiles_per_batch, ctx_len):
    tm, d = s_ref.shape
    g = g_ref[...]
    mod = mod_ref[...]
    sub = ROW_BLOCK // 4
    gains = {True: jnp.broadcast_to(g * (1.0 + mod[4:5]), (sub, d)), False: jnp.broadcast_to(g * (1.0 + mod[1:2]), (sub, d))}
    shifts = {True: jnp.broadcast_to(mod[3:4], (sub, d)), False: jnp.broadcast_to(mod[0:1], (sub, d))}

    def body(r, is_ctx):
        for k in range(4):
            rk = r + k * sub
            x = s_ref[pl.ds(rk, sub), :]
            rs = lax.rsqrt(jnp.mean(x * x, axis=-1, keepdims=True) + EPS)
            h_ref[pl.ds(rk, sub), :] = _bf(x * rs * gains[is_ctx] + shifts[is_ctx])

    _row_block_loops(tm, tile_idx, tiles_per_batch, ctx_len, body)


def _residual_store(s_ref, mod_ref, o_ref, weight, tile_idx, tiles_per_batch, ctx_len):
    tm, d = s_ref.shape
    mod = mod_ref[...]
    gates = {True: jnp.broadcast_to(weight * mod[5:6], (ROW_BLOCK, d)), False: jnp.broadcast_to(weight * mod[2:3], (ROW_BLOCK, d))}

    def body(r, is_ctx):
        rows = pl.ds(r, ROW_BLOCK)
        o_ref[rows, :] = s_ref[rows, :] + gates[is_ctx] * o_ref[rows, :]

    _row_block_loops(tm, tile_idx, tiles_per_batch, ctx_len, body)


def _ffn_kernel(s_ref, g_ref, mod_ref, wg_ref, wu_ref, wd_ref, o_ref, h_ref, *, tiles_per_batch, ctx_len):
    i = pl.program_id(0)
    f = pl.program_id(1)

    @pl.when(f == 0)
    def _():
        _norm_mod_store(s_ref, g_ref, mod_ref, h_ref, i, tiles_per_batch, ctx_len)
        o_ref[...] = jnp.zeros_like(o_ref)

    h = h_ref[...]
    a = jnp.dot(h, wg_ref[...], preferred_element_type=f32)
    u = jnp.dot(h, wu_ref[...], preferred_element_type=f32)
    o_ref[...] += _dot(_silu(a) * u, wd_ref[...])

    @pl.when(f == pl.num_programs(1) - 1)
    def _():
        _residual_store(s_ref, mod_ref, o_ref, FFN_RES, i, tiles_per_batch, ctx_len)


def _ffn(s, g, mods, wg, wu, wd, layer, sub, which, lb, ctx_len):
    t, d = s.shape
    dff = wg.shape[-1]
    tm = _largest_tile(lb, SUB, 576)
    tf = _largest_tile(dff, LANES, 512)
    tpb = lb // tm
    kern = functools.partial(_ffn_kernel, tiles_per_batch=tpb, ctx_len=ctx_len)
    return pl.pallas_call(
        kern,
        grid=(t // tm, dff // tf),
        in_specs=[pl.BlockSpec((tm, d), lambda i, f: (i, 0)),
                  pl.BlockSpec((None, None, 1, d), lambda i, f: (layer, sub, 0, 0)),
                  pl.BlockSpec((None, None, None, 8, d), lambda i, f: (layer, sub, i // tpb, 0, 0)),
                  pl.BlockSpec((None, None, d, tf), lambda i, f: (layer, which, 0, f)),
                  pl.BlockSpec((None, None, d, tf), lambda i, f: (layer, which, 0, f)),
                  pl.BlockSpec((None, None, tf, d), lambda i, f: (layer, which, f, 0))],
        out_specs=pl.BlockSpec((tm, d), lambda i, f: (i, 0)),
        out_shape=jax.ShapeDtypeStruct((t, d), f32),
        scratch_shapes=[pltpu.VMEM((tm, d), bf16)],
        compiler_params=_cparams("parallel", "arbitrary"),
    )(s, g, mods, wg, wu, wd)


def _proj_kernel(s_ref, g_ref, mod_ref, wl_ref, wg_ref, wab_ref, o_ref, oab_ref, h_ref, *, tiles_per_batch, ctx_len,
                 n_gate):
    j = pl.program_id(1)

    @pl.when(j == 0)
    def _():
        _norm_mod_store(s_ref, g_ref, mod_ref, h_ref, pl.program_id(0), tiles_per_batch, ctx_len)
        oab_ref[...] = _dot_nt(h_ref[...], wab_ref[...])

    @pl.when(j < n_gate)
    def _():
        o_ref[...] = _dot_nt(h_ref[...], wg_ref[...])

    @pl.when(j >= n_gate)
    def _():
        o_ref[...] = _dot_nt(h_ref[...], wl_ref[...])


def _proj(s, g, mods, wt, wt_tail, layer, lb, ctx_len, n_lead):
    t, d = s.shape
    tm = _largest_tile(lb, SUB, 768)
    tn = LANES
    while tn * 2 <= 1024 and d % (tn * 2) == 0 and n_lead % (tn * 2) == 0:
        tn *= 2
    tpb = lb // tm
    n_gate = 2 * d // tn
    n_out = 2 * d + n_lead
    kern = functools.partial(_proj_kernel, tiles_per_batch=tpb, ctx_len=ctx_len, n_gate=n_gate)
    return pl.pallas_call(
        kern,
        grid=(t // tm, n_out // tn),
        in_specs=[pl.BlockSpec((tm, d), lambda i, j: (i, 0)),
                  pl.BlockSpec((None, None, 1, d), lambda i, j: (layer, 1, 0, 0)),
                  pl.BlockSpec((None, None, None, 8, d), lambda i, j: (layer, 1, i // tpb, 0, 0)),
                  pl.BlockSpec((None, tn, d), lambda i, j: (layer, jnp.maximum(j - n_gate, 0), 0)),
                  pl.BlockSpec((None, tn, d), lambda i, j: (layer, jnp.minimum(j, n_gate - 1), 0)),
                  pl.BlockSpec((None, LANES, d), lambda i, j: (layer, 2 * d // LANES, 0))],
        out_specs=[pl.BlockSpec((tm, tn), lambda i, j: (i, j)),
                   pl.BlockSpec((tm, LANES), lambda i, j: (i, 0))],
        out_shape=[jax.ShapeDtypeStruct((t, n_out), f32), jax.ShapeDtypeStruct((t, LANES), f32)],
        scratch_shapes=[pltpu.VMEM((tm, d), bf16)],
        compiler_params=_cparams("parallel", "arbitrary"),
    )(s, g, mods, wt, wt_tail, wt_tail)


def _chunk_index(i, n_ctx, n_chunks, reverse):
    if not reverse:
        return i
    return jnp.where(i < n_ctx, n_ctx - 1 - i, n_chunks + n_ctx - 1 - i)


def _group_size(n_chunks, n_ctx):
    for g in (4, 2):
        if n_chunks % g == 0 and n_ctx % g == 0:
            return g
    return 1


def _causal_masks(c, reverse):
    t = lax.broadcasted_iota(jnp.int32, (c, c), 0)
    s = lax.broadcasted_iota(jnp.int32, (c, c), 1)
    if reverse:
        return s >= t, s > t, t, s
    return s <= t, s < t, t, s


def _head_rms(o, g):
    return o * lax.rsqrt(jnp.mean(o * o, axis=-1, keepdims=True) + EPS) * g


def _hgrn_group(items, sts, consts, between=()):
    c = items[0]["q"].shape[0]
    sts = list(sts)
    pending = list(between)

    def run_between():
        if pending:
            pending.pop(0)()

    for x in items:
        x["b"] = _dot_sel(consts[x["d"]][0], x["lc"])
    for x in items:
        d, b = x["d"], x["b"]
        b_end = b[0:1] if d else b[c - 1:c]
        x["o"] = _dot_nt(x["q"] * jnp.exp(b), sts[d])
        sts[d] = sts[d] * jnp.exp(b_end) + _dot_tn(x["v"], x["k"] * jnp.exp(b_end - b))
    run_between()

    for x in items:
        x["attn"] = jnp.zeros((c, c), f32)
    m, level = c // 2, 0
    while m >= SUB:
        n = c // m
        zeros_blk = jnp.zeros((m, LANES), f32)
        for x in items:
            d, b, q, k = x["d"], x["b"], x["q"], x["k"]
            qs, ks = [], []
            for i in range(n):
                lo, hi = i * m, (i + 1) * m
                if (i % 2 == 1) != bool(d):
                    edge = b[hi - 1:hi] if d else b[lo:lo + 1]
                    qs.append(q[lo:hi] * jnp.exp(jnp.minimum(b[lo:hi] - edge, 0.0)))
                    ks.append(zeros_blk)
                else:
                    edge = b[lo - 1:lo] if d else b[hi:hi + 1]
                    ks.append(k[lo:hi] * jnp.exp(jnp.minimum(edge - b[lo:hi], 0.0)))
                    qs.append(zeros_blk)
            x["lv"] = _dot_nt(jnp.concatenate(qs, axis=0), jnp.concatenate(ks, axis=0))
        for x in items:
            x["attn"] = x["attn"] + jnp.where(consts[x["d"]][1][level], x["lv"], 0.0)
        m, level = m // 2, level + 1
        run_between()
    for x in items:
        x["ov"] = _dot(x["attn"], x["v"])

    n = c // SUB
    for x in items:
        sub_mask, ones_red = consts[x["d"]][2:]
        step = SUB - 1 if x["d"] else 1
        q3 = x["q"].reshape(n, SUB, LANES)
        f3 = x["f"].reshape(n, SUB, LANES)
        u = x["k"].reshape(n, SUB, LANES)
        prods = [(q3 * u).reshape(c, LANES)]
        for j in range(1, SUB):
            u = f3 * pltpu.roll(u, step, 1)
            prods.append(jnp.where(sub_mask[j], q3 * u, 0.0).reshape(c, LANES))
        x["sums"] = jnp.dot(_bf(jnp.concatenate(prods, axis=0)), ones_red, preferred_element_type=f32)
    while pending:
        run_between()
    for x in items:
        step = SUB - 1 if x["d"] else 1
        w = x["v"].reshape(n, SUB, LANES)
        od = x["sums"][:c].reshape(n, SUB, LANES) * w
        for j in range(1, SUB):
            w = pltpu.roll(w, step, 1)
            od = od + x["sums"][j * c:(j + 1) * c].reshape(n, SUB, LANES) * w
        x["o"] = x["o"] + x["ov"] + od.reshape(c, LANES)
    return sts


def _hgrn_consts(c, reverse):
    m_incl, _, t, s = _causal_masks(c, reverse)
    level_masks = []
    m = c // 2
    while m >= SUB:
        tb, sb = t // m, s // m
        if reverse:
            level_masks.append((tb % 2 == 0) & (sb == tb + 1))
        else:
            level_masks.append((tb % 2 == 1) & (sb == tb - 1))
        m //= 2
    r = lax.broadcasted_iota(jnp.int32, (1, SUB, LANES), 1)
    sub_mask = [(r <= SUB - 1 - j) if reverse else (r >= j) for j in range(SUB)]
    row3 = lax.broadcasted_iota(jnp.int32, (c, 3 * c), 0)
    lane3 = lax.broadcasted_iota(jnp.int32, (c, 3 * c), 1) % c
    sel3 = jnp.where((lane3 >= row3) if reverse else (lane3 <= row3), 1.0, 0.0).astype(bf16)
    return (sel3, level_masks, sub_mask, jnp.ones((LANES, LANES), bf16))


def _short_conv_tile(u, cw, width, seq):
    r = u.shape[0]
    pos = lax.broadcasted_iota(jnp.int32, (r, 1), 0) % seq
    half = width // 2
    acc = u * cw[half:half + 1]
    for j in range(width):
        if j == half:
            continue
        uj = pltpu.roll(u, (half - j) % r, 0)
        valid = pos >= (half - j) if j < half else pos < seq - (j - half)
        acc = acc + jnp.where(valid, uj, 0.0) * cw[j:j + 1]
    return acc


def _mixers_kernel(alog_ref, dt_ref, hq_ref, hff_ref, hfb_ref, hv_ref, lb_ref, hg_ref,
                   q_ref, k_ref, v_ref, ab_ref, cwq_ref, cwk_ref, cwv_ref, g_ref, oa_ref, o_ref,
                   qn, kn, vn, mt_s, nt_s, qe_s, oe_s, gl_s, *, layer, ctx_len, heads):
    c = CHUNK
    h = pl.program_id(1)
    lb_rows = q_ref.shape[0]
    n_chunks = lb_rows // c
    n_ctx = ctx_len // c
    width = cwq_ref.shape[0]

    def prep(r0, rows, seq):
        for src, cw, dst, mode in ((q_ref, cwq_ref, qn, "q"), (k_ref, cwk_ref, kn, "k"), (v_ref, cwv_ref, vn, "v")):
            y = _silu(_short_conv_tile(src[pl.ds(r0, rows), :], cw[...], width, seq))
            if mode != "v":
                y = y * lax.rsqrt(jnp.sum(y * y, axis=-1, keepdims=True) + EPS)
            if mode == "q":
                y = y * (HEAD_DIM ** -0.5)
            dst[pl.ds(r0, rows), :] = y

    prep(0, ctx_len, ctx_len)
    rows_x = _largest_tile(lb_rows - ctx_len, c, 4 * c)

    def prep_body(i, carry):
        prep(pl.multiple_of(ctx_len + i * rows_x, c), rows_x, c)
        return carry

    lax.fori_loop(0, (lb_rows - ctx_len) // rows_x, prep_body, 0)

    sel_r = lax.broadcasted_iota(jnp.int32, (3 * LANES, 4 * LANES), 0) % LANES
    sel_c = lax.broadcasted_iota(jnp.int32, (3 * LANES, 4 * LANES), 1) // LANES
    ab_sel = jnp.where(sel_r == sel_c * heads + h, 1.0, 0.0).astype(bf16)

    row = lax.broadcasted_iota(jnp.int32, (c, LANES), 0)
    lane = lax.broadcasted_iota(jnp.int32, (c, LANES), 1)
    left_lanes = lane < c
    eye_right = jnp.where(lane == row + c, 1.0, 0.0)
    row3 = lax.broadcasted_iota(jnp.int32, (c, 3 * c), 0)
    lane3 = lax.broadcasted_iota(jnp.int32, (c, 3 * c), 1) % c
    ones3 = jnp.ones((c, 3 * c), bf16)
    zeros_k = jnp.zeros((c, HEAD_DIM), f32)
    dirs = []
    for d in (0, 1):
        if d == 0:
            m_incl, m_strict, m_incl_t, sel3 = lane <= row, lane < row, (row <= lane) & left_lanes, lane3 <= row3
        else:
            m_incl, m_strict, m_incl_t, sel3 = (lane >= row) & left_lanes, (lane > row) & left_lanes, lane <= row, lane3 >= row3
        neg_a = -jnp.exp(jnp.full((1, 1), alog_ref[layer, d, h], f32))
        dirs.append((m_incl, m_strict, m_incl_t, jnp.where(sel3, 1.0, 0.0).astype(bf16), neg_a, dt_ref[layer, d, h]))

    grp = max(g for g in (12, 6, 4, 3, 2, 1) if n_chunks % g == 0)

    def pre_body(it, carry):
        chains = []
        for j in range(grp):
            ci = it * grp + j
            r0 = pl.multiple_of(ci * c, c)
            q = qn[pl.ds(r0, c), :]
            k = kn[pl.ds(r0, c), :]
            kq = _dot_nt(jnp.concatenate([k, q], axis=0), jnp.concatenate([k, zeros_k], axis=0))
            ab = jnp.dot(jnp.concatenate(_split3(ab_ref[pl.ds(r0, c), :]), axis=1), ab_sel,
                         preferred_element_type=f32)
            for d in (0, 1):
                chains.append(dict(d=d, ci=ci, r0=r0, q=q, k=k, kq=kq, a_in=ab[:, d * LANES:(d + 1) * LANES],
                                   b_in=ab[:, (2 + d) * LANES:(3 + d) * LANES]))
        for x in chains:
            _, _, m_incl_t, sel3, neg_a, dt = dirs[x["d"]]
            za = x["a_in"] + dt
            g = neg_a * (jnp.maximum(za, 0.0) + jnp.log(1.0 + jnp.exp(-jnp.abs(za))))
            x["beta"] = _sigmoid(x["b_in"])
            x["gc"] = _dot_sel(sel3, g)
            x["gc_row"] = _dot_sel(ones3, jnp.where(m_incl_t, g, 0.0))
        for x in chains:
            m_incl, m_strict = dirs[x["d"]][:2]
            x["gamma"] = jnp.where(m_incl, jnp.exp(jnp.minimum(x["gc"] - x["gc_row"], 0.0)), 0.0)
            x["x"] = eye_right - jnp.where(m_strict, x["beta"] * x["kq"][:c] * x["gamma"], 0.0)
        span = 1
        while span < c:
            for x in chains:
                xh, xl = _split2(x["x"])
                x["r"] = (jnp.dot(xh[:, :c], jnp.concatenate([xh, xl], axis=1), preferred_element_type=f32),
                          jnp.dot(xl[:, :c], xh, preferred_element_type=f32))
            for x in chains:
                r, r_lo = x["r"]
                x["x"] = r[:, :LANES] + r[:, LANES:] + r_lo + jnp.where(left_lanes, 0.0, x["x"])
            span *= 2
        for x in chains:
            r0 = x["r0"]
            x["egc"] = jnp.exp(x["gc"])
            kb = x["k"] * x["beta"]
            v = vn[pl.ds(r0, c), :]
            ah, al = _split2(x["x"][:, c:])
            bh, bl = _split2(jnp.concatenate([v * x["beta"], kb * x["egc"]], axis=1))
            r = jnp.dot(ah, jnp.concatenate([bh, bl], axis=1), preferred_element_type=f32)
            x["sol"] = r[:, :2 * HEAD_DIM] + r[:, 2 * HEAD_DIM:] + jnp.dot(al, bh, preferred_element_type=f32)
        for x in chains:
            gc = x["gc"]
            x["gc_end"] = gc[0:1] if x["d"] else gc[c - 1:c]
            x["kw"] = _dot_tn(x["k"] * jnp.exp(x["gc_end"] - gc), x["sol"])
            x["qw"] = _dot((x["kq"][c:] * x["gamma"])[:, :c], x["sol"])
        for x in chains:
            d, r0, ci = x["d"], x["r0"], x["ci"]
            nt_s[d, ci] = x["kw"][:, :HEAD_DIM]
            mt_s[d, ci] = -x["kw"][:, HEAD_DIM:]
            oe_s[d, pl.ds(r0, c), :] = x["qw"][:, :HEAD_DIM]
            qe_s[d, pl.ds(r0, c), :] = x["q"] * x["egc"] - x["qw"][:, HEAD_DIM:]
            gl_s[d, ci] = jnp.broadcast_to(x["gc_end"], (SUB, LANES))
        return carry

    lax.fori_loop(0, n_chunks // grp, pre_body, 0)

    o_ref[...] = jnp.zeros_like(o_ref)
    oa_ref[...] = jnp.zeros_like(oa_ref)

    def dn_step(i, sts):
        cis = [_chunk_index(i, n_ctx, n_chunks, d == 1) for d in (0, 1)]
        r0s = [pl.multiple_of(ci * c, c) for ci in cis]
        lin = [_dot(jnp.concatenate([mt_s[d, cis[d]], qe_s[d, pl.ds(r0s[d], c), :]], axis=0), sts[d]) for d in (0, 1)]
        for d in (0, 1):
            o_ref[pl.ds(r0s[d], c), :] += lin[d][HEAD_DIM:] + oe_s[d, pl.ds(r0s[d], c), :]
        return [jnp.exp(gl_s[d, cis[d]][0:1]) * sts[d] + lin[d][:HEAD_DIM] + nt_s[d, cis[d]] for d in (0, 1)]

    raw = lb_ref[...]
    e = jnp.exp(raw - jnp.max(raw, axis=0, keepdims=True))
    sm = e / jnp.sum(e, axis=0, keepdims=True)
    lbs = jnp.sum(sm[:layer + 1], axis=0) - sm[0]
    consts = [_hgrn_consts(c, False), _hgrn_consts(c, True)]
    hgrp = _group_size(n_chunks, n_ctx)

    def body(it, carry):
        dn_sts = list(carry[2:])
        items = []
        for j in range(hgrp):
            for d in (0, 1):
                r0 = pl.multiple_of(_chunk_index(it * hgrp + j, n_ctx, n_chunks, d == 1) * c, c)
                lb = lbs[d:d + 1]
                f = lb + (1.0 - lb) * _sigmoid((hfb_ref if d else hff_ref)[pl.ds(r0, c), :])
                items.append(dict(d=d, r0=r0, q=_silu(hq_ref[pl.ds(r0, c), :]), v=hv_ref[pl.ds(r0, c), :],
                                  f=f, lc=jnp.log(f), k=1.0 - f))

        def scan_step(j):
            def run():
                dn_sts[:] = dn_step(it * hgrp + j, dn_sts)
            return run

        sts = _hgrn_group(items, carry[:2], consts, between=[scan_step(j) for j in range(hgrp)])
        for x in items:
            oa_ref[pl.ds(x["r0"], c), :] += x["o"]
        return (*sts, *dn_sts)

    zero_state = jnp.zeros((HEAD_DIM, HEAD_DIM), f32)
    lax.fori_loop(0, n_chunks // hgrp, body, (zero_state,) * 4)

    o_ref[...] = _head_rms(o_ref[...], g_ref[...])
    oa_ref[...] = _head_rms(oa_ref[...], hg_ref[...])


def _mixers(p, p_ab, lower_bounds, hgrn_norm_g, conv_w, a_log, dt_bias, dn_norm_g, layer, batch, lb_rows, ctx_len, heads,
            off_hq, off_hff, off_hfb, off_hv, off_q, off_k, off_v):
    t = p.shape[0]
    depth, width, _ = conv_w.shape
    kb_cols = heads * HEAD_DIM
    n_chunks = lb_rows // CHUNK

    def col(off):
        return pl.BlockSpec((lb_rows, HEAD_DIM), lambda b, h: (b, off // HEAD_DIM + h))

    def cw(off):
        return pl.BlockSpec((None, width, HEAD_DIM), lambda b, h: (layer, 0, off // HEAD_DIM + h))

    def gain():
        return pl.BlockSpec((None, 1, HEAD_DIM), lambda b, h: (layer, 0, 0))

    smem = pl.BlockSpec(memory_space=pltpu.SMEM)
    kern = functools.partial(_mixers_kernel, layer=layer, ctx_len=ctx_len, heads=heads)
    row_buf = pltpu.VMEM((lb_rows, HEAD_DIM), f32)
    dir_buf = pltpu.VMEM((2, lb_rows, HEAD_DIM), f32)
    state_buf = pltpu.VMEM((2, n_chunks, HEAD_DIM, HEAD_DIM), f32)
    out = jax.ShapeDtypeStruct((t, heads * HEAD_DIM), f32)
    return pl.pallas_call(
        kern,
        grid=(batch, heads),
        in_specs=[smem, smem, col(off_hq), col(off_hff), col(off_hfb), col(off_hv),
                  pl.BlockSpec((depth, 2, HEAD_DIM), lambda b, h: (0, 0, h)), gain(),
                  col(off_q), col(off_k), col(off_v),
                  pl.BlockSpec((lb_rows, LANES), lambda b, h: (b, 0)),
                  cw(0), cw(kb_cols), cw(2 * kb_cols), gain()],
        out_specs=[pl.BlockSpec((lb_rows, HEAD_DIM), lambda b, h: (b, h))] * 2,
        out_shape=[out, out],
        scratch_shapes=[row_buf] * 3 + [state_buf] * 2 + [dir_buf] * 2 + [pltpu.VMEM((2, n_chunks, SUB, LANES), f32)],
        compiler_params=_cparams("parallel", "parallel"),
    )(a_log, dt_bias, p, p, p, p, lower_bounds, hgrn_norm_g.reshape(depth, 1, HEAD_DIM),
      p, p, p, p_ab, conv_w, conv_w, conv_w, dn_norm_g.reshape(depth, 1, HEAD_DIM))


def _merge_kernel(s_ref, mod_ref, oa_ref, ob_ref, ag_ref, bz_ref, ga_ref, gb_ref, wa_ref, wb_ref, wo_ref, o_ref,
                  *, tiles_per_batch, ctx_len):
    is_ctx = _is_ctx_rows(pl.program_id(0), tiles_per_batch, s_ref.shape[0], ctx_len)
    ya = _bf(oa_ref[...] * _silu(ag_ref[...]))
    yb = _bf(ob_ref[...] * _silu(bz_ref[...]))
    y = (_sigmoid(ga_ref[...]) * jnp.dot(ya, wa_ref[...], preferred_element_type=f32)
         + _sigmoid(gb_ref[...]) * jnp.dot(yb, wb_ref[...], preferred_element_type=f32))
    mod = mod_ref[...]
    gate = jnp.where(is_ctx, mod[5:6], mod[2:3])
    o_ref[...] = s_ref[...] + gate * _dot(y, wo_ref[...])


def _merge(s, mods, oa, ob, p, wa, wb, wo, layer, lb, ctx_len, off_ag, off_bz):
    t, d = s.shape
    wa_cols = oa.shape[1]
    wb_cols = ob.shape[1]
    tm = _largest_tile(lb, SUB, 256)
    tpb = lb // tm
    kern = functools.partial(_merge_kernel, tiles_per_batch=tpb, ctx_len=ctx_len)

    def whole(a):
        return pl.BlockSpec((None,) + a.shape[1:], lambda i: (layer, 0, 0))

    return pl.pallas_call(
        kern,
        grid=(t // tm,),
        in_specs=[pl.BlockSpec((tm, d), lambda i: (i, 0)),
                  pl.BlockSpec((None, None, None, 8, d), lambda i: (layer, 1, i // tpb, 0, 0)),
                  pl.BlockSpec((tm, wa_cols), lambda i: (i, 0)),
                  pl.BlockSpec((tm, wb_cols), lambda i: (i, 0)),
                  pl.BlockSpec((tm, wa_cols), lambda i: (i, off_ag // wa_cols)),
                  pl.BlockSpec((tm, wb_cols), lambda i: (i, off_bz // wb_cols)),
                  pl.BlockSpec((tm, d), lambda i: (i, 0)),
                  pl.BlockSpec((tm, d), lambda i: (i, 1)),
                  whole(wa), whole(wb), whole(wo)],
        out_specs=pl.BlockSpec((tm, d), lambda i: (i, 0)),
        out_shape=jax.ShapeDtypeStruct((t, d), f32),
        compiler_params=_cparams("parallel"),
    )(s, mods, oa, ob, p, p, p, p, wa, wb, wo)


def _final_kernel(s_ref, g_ref, o_ref):
    s = s_ref[...]
    o_ref[...] = s * lax.rsqrt(jnp.mean(s * s, axis=-1, keepdims=True) + EPS) * g_ref[...]


def _final_norm(s3, g, ctx_len, seq):
    batch, _, d = s3.shape
    tm = _largest_tile(ctx_len, SUB, 256)
    return pl.pallas_call(
        _final_kernel,
        grid=(batch, seq // tm),
        in_specs=[pl.BlockSpec((None, tm, d), lambda b, j: (b, ctx_len // tm + j, 0)),
                  pl.BlockSpec((1, d), lambda b, j: (0, 0))],
        out_specs=pl.BlockSpec((None, tm, d), lambda b, j: (b, j, 0)),
        out_shape=jax.ShapeDtypeStruct((batch, seq, d), f32),
        compiler_params=_cparams("parallel", "parallel"),
    )(s3, g.reshape(1, d))


def kernel(x, c, ctx, c_ctx, w_ada, b_ada, norm_g, final_norm_g, ffn_w_gate, ffn_w_up, ffn_w_down, w_in,
           hgrn_lower_bounds, hgrn_norm_g, dn_conv_w, dn_a_log, dn_dt_bias, dn_norm_g, w_branch_a, w_branch_b, w_out):
    batch, seq, d = x.shape
    ctx_len = ctx.shape[1]
    depth = w_ada.shape[0]
    lb = ctx_len + seq
    ka = hgrn_lower_bounds.shape[-1]
    wa = w_branch_a.shape[1]
    wb = w_branch_b.shape[1]
    kb = (dn_conv_w.shape[-1] - wb) // 2
    heads = dn_a_log.shape[-1]
    assert ka == wa == kb == wb == heads * HEAD_DIM and seq % CHUNK == 0 and ctx_len % CHUNK == 0
    assert d % wa == 0 and 4 * heads <= LANES and batch + 1 <= 8 and ctx_len % ROW_BLOCK == 0 and lb % ROW_BLOCK == 0

    cc = jnp.concatenate([c, c_ctx[None], jnp.zeros((8 - batch - 1, d), f32)], axis=0)
    m = _adaln(cc, w_ada, b_ada).reshape(depth, 8, 3, 3, d)
    m_own = jnp.transpose(m[:, :batch], (0, 2, 1, 3, 4))
    m_ctx = jnp.broadcast_to(m[:, batch][:, :, None], (depth, 3, batch, 3, d))
    mods = jnp.concatenate([m_own, m_ctx, jnp.zeros((depth, 3, batch, 2, d), f32)], axis=3)
    norm_g4 = norm_g.reshape(depth, 3, 1, d)

    o_ba = 3 * ka + 2 * wa + 2 * kb + wb + wb
    wt = jnp.swapaxes(w_in, 1, 2).astype(bf16)
    wt_tail = jnp.concatenate([wt[:, o_ba + 4 * heads:], wt[:, o_ba:o_ba + 4 * heads],
                               jnp.zeros((depth, LANES - 4 * heads, d), bf16)], axis=1)
    base = 2 * d
    off_q, off_ff, off_fb, off_v, off_ag = base, base + ka, base + 2 * ka, base + 3 * ka, base + 3 * ka + wa
    off_bq = base + 3 * ka + 2 * wa
    off_bk, off_bv, off_bz = off_bq + kb, off_bq + 2 * kb, off_bq + 2 * kb + wb

    wg, wu, wd = ffn_w_gate.astype(bf16), ffn_w_up.astype(bf16), ffn_w_down.astype(bf16)
    wbr_a, wbr_b, wo = w_branch_a.astype(bf16), w_branch_b.astype(bf16), w_out.astype(bf16)

    s = jnp.concatenate([ctx, x], axis=1).reshape(batch * lb, d)
    for l in range(depth):
        s = _ffn(s, norm_g4, mods, wg, wu, wd, l, 0, 0, lb, ctx_len)
        p, p_ab = _proj(s, norm_g4, mods, wt, wt_tail, l, lb, ctx_len, o_ba)
        oa, ob = _mixers(p, p_ab, hgrn_lower_bounds, hgrn_norm_g, dn_conv_w, dn_a_log, dn_dt_bias, dn_norm_g, l, batch, lb,
                         ctx_len, heads, off_q, off_ff, off_fb, off_v, off_bq, off_bk, off_bv)
        s = _merge(s, mods, oa, ob, p, wbr_a, wbr_b, wo, l, lb, ctx_len, off_ag, off_bz)
        s = _ffn(s, norm_g4, mods, wg, wu, wd, l, 2, 1, lb, ctx_len)
    return _final_norm(s.reshape(batch, lb, d), final_norm_g, ctx_len, seq)
```
